```python
import jax, jax.numpy as jnp
from jax import lax
import numpy as np

D_MODEL = 4096
BATCH = 32
SEQ = 256
DEPTH = 2
DEC_BATCH = 8
DEC_SEQ = 4096
PAST_LEN = 256

GRID_W = 64
FOURIER_WIDTH = D_MODEL // 2
FOURIER_GROUPS = 8
FOURIER_GROUP_CH = FOURIER_WIDTH // FOURIER_GROUPS
LRU_WIDTH = D_MODEL // 2
LRU_HEADS = 16
LRU_BLOCK = LRU_WIDTH // LRU_HEADS
CONV_WIDTH = 4
CONV_LEFT = 1
LRU_C = 8.0
N_BRANCH = 2
IN_WIDTH = 2 * FOURIER_WIDTH + 2 * LRU_WIDTH + N_BRANCH * D_MODEL
IN_SPLITS = [FOURIER_WIDTH, 2 * FOURIER_WIDTH, 2 * FOURIER_WIDTH + LRU_WIDTH,
             2 * FOURIER_WIDTH + 2 * LRU_WIDTH, 2 * FOURIER_WIDTH + 2 * LRU_WIDTH + D_MODEL]
EPS = 1e-6

kernel_name = "gated_fourier_rglru_diffusion_step"


def rms_norm(x, g):
    xf = x.astype(jnp.float32)
    y = xf * lax.rsqrt(jnp.mean(xf * xf, axis=-1, keepdims=True) + EPS)
    return (y * g.astype(jnp.float32)).astype(x.dtype)


def fourier_mix(u, w_fmix):
    B, T, _ = u.shape
    ug = u.reshape(B, T, FOURIER_GROUPS, FOURIER_GROUP_CH).astype(jnp.float32)
    f = jnp.fft.fftn(ug, axes=(1, 3), norm="ortho").real
    return f.reshape(B, T, FOURIER_WIDTH).astype(u.dtype) @ w_fmix


def depthwise_conv(x, w, b):
    T = x.shape[1]
    xp = jnp.pad(x, ((0, 0), (CONV_LEFT, CONV_WIDTH - 1 - CONV_LEFT), (0, 0)))
    y = b
    for k in range(CONV_WIDTH):
        y = y + xp[:, k:k + T] * w[k]
    return y


def block_diag_linear(x, w, b):
    B, T, _ = x.shape
    xb = x.reshape(B, T, LRU_HEADS, LRU_BLOCK)
    y = jnp.einsum("bthi,hij->bthj", xb, w).reshape(B, T, LRU_WIDTH)
    return y + b


def linear_scan(a, b, h0, reverse):
    idx = -1 if reverse else 0
    b = b.at[:, idx].add(a[:, idx] * h0)

    def combine(left, right):
        a_l, b_l = left
        a_r, b_r = right
        return a_l * a_r, a_r * b_l + b_r

    _, h = lax.associative_scan(combine, (a, b), axis=1, reverse=reverse)
    return h


def rg_lru_direction(xc, w_a, b_a, w_x, b_x, lam, h0, reverse):
    gate_a = jax.nn.sigmoid(block_diag_linear(xc, w_a, b_a).astype(jnp.float32))
    gate_x = jax.nn.sigmoid(block_diag_linear(xc, w_x, b_x).astype(jnp.float32))
    log_a = -LRU_C * gate_a * jax.nn.softplus(-lam.astype(jnp.float32))
    a = jnp.exp(log_a)
    u = jnp.sqrt(-jnp.expm1(2.0 * log_a)) * (gate_x * xc.astype(jnp.float32))
    return linear_scan(a, u, h0.astype(jnp.float32), reverse)


def layer(x, cond, h0, norm_g, w_ada, b_ada, w_in, w_fmix, conv_w, conv_b,
          w_gate_a, b_gate_a, w_gate_x, b_gate_x, lam, w_branch_f, w_branch_r, w_out):
    mod = jax.nn.silu(cond) @ w_ada + b_ada
    shift, scale, gate = jnp.split(mod[:, None, :], 3, axis=-1)
    h = rms_norm(x, norm_g) * (1.0 + scale) + shift
    z = h @ w_in
    f_in, f_gate, r_in, r_gate, g_f, g_r = jnp.split(z, IN_SPLITS, axis=-1)
    y_f = fourier_mix(f_in, w_fmix) * jax.nn.silu(f_gate)
    xc = depthwise_conv(r_in, conv_w, conv_b)
    h_fwd = rg_lru_direction(xc, w_gate_a[0], b_gate_a[0], w_gate_x[0], b_gate_x[0],
                             lam[0], h0[:, 0], False)
    h_bwd = rg_lru_direction(xc, w_gate_a[1], b_gate_a[1], w_gate_x[1], b_gate_x[1],
                             lam[1], h0[:, 1], True)
    y_r = (h_fwd + h_bwd).astype(x.dtype) * jax.nn.silu(r_gate)
    merged = jax.nn.sigmoid(g_f) * (y_f @ w_branch_f) + jax.nn.sigmoid(g_r) * (y_r @ w_branch_r)
    x = x + gate * (merged @ w_out)
    final_states = jnp.stack([h_fwd[:, -1], h_bwd[:, 0]], axis=1).astype(x.dtype)
    return x, final_states


def setup_inputs(seed: int = 0) -> dict:
    key = jax.random.key(seed)
    ks = jax.random.split(key, 24)
    f32 = jnp.float32

    def nrm(k, shape, scale):
        return jax.random.normal(k, shape, f32) * scale

    a0 = jax.random.uniform(ks[13], (DEPTH, 2, LRU_WIDTH), f32, 0.9, 0.999)
    p = a0 ** (1.0 / LRU_C)
    lam = jnp.log(p) - jnp.log1p(-p)
    return {
        "x_prompt": nrm(ks[0], (BATCH, SEQ, D_MODEL), 1.0),
        "x_sample": nrm(ks[1], (DEC_BATCH, DEC_SEQ, D_MODEL), 1.0),
        "state_lru": nrm(ks[2], (DEC_BATCH, DEPTH, 2, LRU_WIDTH), 0.5),
        "c": nrm(ks[3], (DEC_BATCH, D_MODEL), 1.0),
        "c_ctx": nrm(ks[4], (D_MODEL,), 1.0),
        "norm_g": 1.0 + nrm(ks[5], (DEPTH, D_MODEL), 0.01),
        "w_ada": nrm(ks[6], (DEPTH, D_MODEL, 3 * D_MODEL), D_MODEL ** -0.5),
        "b_ada": nrm(ks[7], (DEPTH, 3 * D_MODEL), 0.01),
        "w_in": nrm(ks[8], (DEPTH, D_MODEL, IN_WIDTH), D_MODEL ** -0.5),
        "w_fmix": nrm(ks[9], (DEPTH, FOURIER_WIDTH, FOURIER_WIDTH), FOURIER_WIDTH ** -0.5),
        "conv_w": nrm(ks[10], (DEPTH, CONV_WIDTH, LRU_WIDTH), 0.5),
        "conv_b": nrm(ks[11], (DEPTH, LRU_WIDTH), 0.01),
        "w_gate_a": nrm(ks[12], (DEPTH, 2, LRU_HEADS, LRU_BLOCK, LRU_BLOCK), LRU_BLOCK ** -0.5),
        "b_gate_a": nrm(ks[14], (DEPTH, 2, LRU_WIDTH), 0.01),
        "w_gate_x": nrm(ks[15], (DEPTH, 2, LRU_HEADS, LRU_BLOCK, LRU_BLOCK), LRU_BLOCK ** -0.5),
        "b_gate_x": nrm(ks[16], (DEPTH, 2, LRU_WIDTH), 0.01),
        "lam": lam,
        "w_branch_f": nrm(ks[17], (DEPTH, FOURIER_WIDTH, D_MODEL), FOURIER_WIDTH ** -0.5),
        "w_branch_r": nrm(ks[18], (DEPTH, LRU_WIDTH, D_MODEL), LRU_WIDTH ** -0.5),
        "w_out": nrm(ks[19], (DEPTH, D_MODEL, D_MODEL), D_MODEL ** -0.5),
        "final_g": 1.0 + nrm(ks[20], (D_MODEL,), 0.01),
    }


def reference(x_prompt, x_sample, state_lru, c, c_ctx, norm_g, w_ada, b_ada, w_in, w_fmix,
              conv_w, conv_b, w_gate_a, b_gate_a, w_gate_x, b_gate_x, lam,
              w_branch_f, w_branch_r, w_out, final_g):
    x_ctx = x_prompt
    cond_ctx = c_ctx[None, :]
    h0_ctx = jnp.zeros((x_prompt.shape[0], 2, LRU_WIDTH), x_prompt.dtype)
    ctx_states = []
    for l in range(DEPTH):
        x_ctx, st = layer(x_ctx, cond_ctx, h0_ctx, norm_g[l], w_ada[l], b_ada[l], w_in[l],
                          w_fmix[l], conv_w[l], conv_b[l], w_gate_a[l], b_gate_a[l],
                          w_gate_x[l], b_gate_x[l], lam[l], w_branch_f[l], w_branch_r[l],
                          w_out[l])
        ctx_states.append(st)
    y_prompt = rms_norm(x_ctx, final_g)
    new_state_lru = jnp.stack(ctx_states, axis=1)

    x_lat = x_sample
    for l in range(DEPTH):
        x_lat, _ = layer(x_lat, c, state_lru[:, l], norm_g[l], w_ada[l], b_ada[l], w_in[l],
                         w_fmix[l], conv_w[l], conv_b[l], w_gate_a[l], b_gate_a[l],
                         w_gate_x[l], b_gate_x[l], lam[l], w_branch_f[l], w_branch_r[l],
                         w_out[l])
    y_sample = rms_norm(x_lat, final_g)
    return (y_prompt, y_sample, new_state_lru)
```

```python
import functools
import math

import jax
import jax.numpy as jnp
from jax import lax
from jax.experimental import pallas as pl
from jax.experimental.pallas import tpu as pltpu

F32 = jnp.float32
BF16 = jnp.bfloat16

D_MODEL = 4096
DEPTH = 2
FOURIER_WIDTH = D_MODEL // 2
FOURIER_GROUP_CH = 256
LRU_WIDTH = D_MODEL // 2
LRU_BLOCK = 128
LRU_HEADS = LRU_WIDTH // LRU_BLOCK
CONV_WIDTH = 4
LRU_C = 8.0
IN_WIDTH = 2 * FOURIER_WIDTH + 2 * LRU_WIDTH + 2 * D_MODEL
EPS = 1e-6

SUBLANES = 8
LANES = 128
VMEM_LIMIT_BYTES = 58 * 1024 * 1024

COND_ROWS = 16
COL_BLOCK = 1024
SLOT_AC, SLOT_SF, SLOT_SR, SLOT_SGF, SLOT_SGR = 0, 2, 4, 6, 10
PACKED_BLOCKS = 14


def _sigmoid(x):
    return 1.0 / (1.0 + jnp.exp(-x))


def _params(*sem):
    return pltpu.CompilerParams(dimension_semantics=sem, vmem_limit_bytes=VMEM_LIMIT_BYTES)


def _ada_kernel(c_ref, w_ref, b_ref, o_ref):
    c = c_ref[...]
    s = (c * _sigmoid(c)).astype(BF16)
    w = w_ref[...].astype(BF16)
    o_ref[...] = jnp.dot(s, w, preferred_element_type=F32) + b_ref[...]


def _ada(cond, w_ada, b_ada):
    tn = 512
    n_out = 3 * D_MODEL
    return pl.pallas_call(
        _ada_kernel,
        grid=(DEPTH, n_out // tn),
        in_specs=[
            pl.BlockSpec((COND_ROWS, D_MODEL), lambda l, j: (0, 0)),
            pl.BlockSpec((None, D_MODEL, tn), lambda l, j: (l, 0, j)),
            pl.BlockSpec((None, 1, tn), lambda l, j: (l, 0, j)),
        ],
        out_specs=pl.BlockSpec((None, COND_ROWS, tn), lambda l, j: (l, 0, j)),
        out_shape=jax.ShapeDtypeStruct((DEPTH, COND_ROWS, n_out), F32),
        compiler_params=_params("parallel", "parallel"),
        name="ada_mod",
    )(cond, w_ada, b_ada.reshape(DEPTH, 1, n_out))


def _norm_mod_kernel(x_ref, g_ref, sc_ref, sh_ref, o_ref):
    x = x_ref[...]
    ms = jnp.mean(x * x, axis=-1, keepdims=True)
    y = (x * lax.rsqrt(ms + EPS)) * g_ref[...]
    o_ref[...] = (y * (1.0 + sc_ref[...]) + sh_ref[...]).astype(o_ref.dtype)


def _norm_kernel(x_ref, g_ref, o_ref):
    x = x_ref[...]
    ms = jnp.mean(x * x, axis=-1, keepdims=True)
    o_ref[...] = ((x * lax.rsqrt(ms + EPS)) * g_ref[...]).astype(o_ref.dtype)


def _norm_mod(x, g, scale, shift, row_of, tm):
    n = x.shape[0]
    return pl.pallas_call(
        _norm_mod_kernel,
        grid=(n // tm,),
        in_specs=[
            pl.BlockSpec((tm, D_MODEL), lambda i: (i, 0)),
            pl.BlockSpec((1, D_MODEL), lambda i: (0, 0)),
            pl.BlockSpec((None, 1, D_MODEL), lambda i: (row_of(i), 0, 0)),
            pl.BlockSpec((None, 1, D_MODEL), lambda i: (row_of(i), 0, 0)),
        ],
        out_specs=pl.BlockSpec((tm, D_MODEL), lambda i: (i, 0)),
        out_shape=jax.ShapeDtypeStruct((n, D_MODEL), BF16),
        compiler_params=_params("parallel"),
        name="norm_mod",
    )(x, g, scale, shift)


def _final_norm(x, g, tm):
    n = x.shape[0]
    return pl.pallas_call(
        _norm_kernel,
        grid=(n // tm,),
        in_specs=[
            pl.BlockSpec((tm, D_MODEL), lambda i: (i, 0)),
            pl.BlockSpec((1, D_MODEL), lambda i: (0, 0)),
        ],
        out_specs=pl.BlockSpec((tm, D_MODEL), lambda i: (i, 0)),
        out_shape=jax.ShapeDtypeStruct((n, D_MODEL), F32),
        compiler_params=_params("parallel"),
        name="final_norm",
    )(x, g)


_ROW_CHUNK = 256


def _inproj_kernel(h_ref, w_ref, cs_ref, z16_ref, as_ref, rin_ref, acc_ref, *, tm):
    j = pl.program_id(1)
    acc_ref[...] = jnp.dot(h_ref[...], w_ref[...], preferred_element_type=F32)
    n_chunks = tm // _ROW_CHUNK

    def rows(c):
        return pl.ds(pl.multiple_of(c * _ROW_CHUNK, _ROW_CHUNK), _ROW_CHUNK)

    @pl.when(j < 2)
    def _():
        def body(c, carry):
            r = rows(c)
            zb = acc_ref[r, :].astype(BF16)
            for g in range(COL_BLOCK // FOURIER_GROUP_CH):
                cols = slice(g * FOURIER_GROUP_CH, (g + 1) * FOURIER_GROUP_CH)
                t = jnp.dot(zb[:, cols], cs_ref[...], preferred_element_type=F32)
                z16_ref[r, cols] = t[:, :FOURIER_GROUP_CH].astype(BF16)
                as_ref[r, cols] = t[:, FOURIER_GROUP_CH:].astype(BF16)
            return carry
        lax.fori_loop(0, n_chunks, body, 0)

    @pl.when(((j >= 2) & (j < 4)) | ((j >= 6) & (j < 8)))
    def _():
        def body(c, carry):
            r = rows(c)
            z = acc_ref[r, :]
            z16_ref[r, :] = (z * _sigmoid(z)).astype(BF16)
            return carry
        lax.fori_loop(0, n_chunks, body, 0)

    @pl.when((j >= 4) & (j < 6))
    def _():
        rin_ref[...] = acc_ref[...]

    @pl.when(j >= 8)
    def _():
        def body(c, carry):
            r = rows(c)
            z16_ref[r, :] = _sigmoid(acc_ref[r, :]).astype(BF16)
            return carry
        lax.fori_loop(0, n_chunks, body, 0)


def _packed_slot(j):
    return jnp.where(j < 4, j, jnp.where(j < 6, 3, j - 2))


def _inproj(h, w_in, cs, tm):
    n = h.shape[0]
    tn = COL_BLOCK
    return pl.pallas_call(
        functools.partial(_inproj_kernel, tm=tm),
        grid=(n // tm, IN_WIDTH // tn),
        in_specs=[
            pl.BlockSpec((tm, D_MODEL), lambda i, j: (i, 0), pipeline_mode=pl.Buffered(1)),
            pl.BlockSpec((D_MODEL, tn), lambda i, j: (0, j)),
            pl.BlockSpec((FOURIER_GROUP_CH, 2 * FOURIER_GROUP_CH), lambda i, j: (0, 0)),
        ],
        out_specs=[
            pl.BlockSpec((tm, tn), lambda i, j: (i, _packed_slot(j))),
            pl.BlockSpec((tm, tn), lambda i, j: (i, jnp.minimum(j, 1))),
            pl.BlockSpec((tm, tn), lambda i, j: (i, jnp.clip(j - 4, 0, 1))),
        ],
        out_shape=[
            jax.ShapeDtypeStruct((n, PACKED_BLOCKS * COL_BLOCK), BF16),
            jax.ShapeDtypeStruct((n, FOURIER_WIDTH), BF16),
            jax.ShapeDtypeStruct((n, LRU_WIDTH), F32),
        ],
        scratch_shapes=[pltpu.VMEM((tm, tn), F32)],
        compiler_params=_params("parallel", "arbitrary"),
        name="in_proj",
    )(h, w_in, cs)


def _tdft_kernel(c_ref, s_ref, ac_ref, as_ref, o_ref):
    o_ref[...] = (jnp.dot(c_ref[...], ac_ref[...], preferred_element_type=F32)
                  + jnp.dot(s_ref[...], as_ref[...], preferred_element_type=F32)).astype(BF16)


def _tdft_multi_kernel(c_ref, s_ref, ac_ref, as_ref, o_ref, *, nb, seq):
    for b in range(nb):
        r = slice(b * seq, (b + 1) * seq)
        o_ref[r, :] = (jnp.dot(c_ref[...], ac_ref[r, :], preferred_element_type=F32)
                       + jnp.dot(s_ref[...], as_ref[r, :], preferred_element_type=F32)).astype(BF16)


def _tdft(z16, a_s, cmat, smat_neg, batch, seq):
    n = batch * seq
    tn = COL_BLOCK
    out_shape = jax.ShapeDtypeStruct((n, FOURIER_WIDTH), BF16)
    if seq >= 1024:
        tm = 512
        steps = seq // tm
        return pl.pallas_call(
            _tdft_kernel,
            grid=(batch, FOURIER_WIDTH // tn, steps),
            in_specs=[
                pl.BlockSpec((tm, seq), lambda b, c, i: (i, 0)),
                pl.BlockSpec((tm, seq), lambda b, c, i: (i, 0)),
                pl.BlockSpec((seq, tn), lambda b, c, i: (b, SLOT_AC + c)),
                pl.BlockSpec((seq, tn), lambda b, c, i: (b, c)),
            ],
            out_specs=pl.BlockSpec((tm, tn), lambda b, c, i: (b * steps + i, c)),
            out_shape=out_shape,
            compiler_params=_params("parallel", "parallel", "parallel"),
            name="pos_dft",
        )(cmat, smat_neg, z16, a_s)
    nb = 8
    return pl.pallas_call(
        functools.partial(_tdft_multi_kernel, nb=nb, seq=seq),
        grid=(batch // nb, FOURIER_WIDTH // tn),
        in_specs=[
            pl.BlockSpec((seq, seq), lambda b, c: (0, 0)),
            pl.BlockSpec((seq, seq), lambda b, c: (0, 0)),
            pl.BlockSpec((nb * seq, tn), lambda b, c: (b, SLOT_AC + c)),
            pl.BlockSpec((nb * seq, tn), lambda b, c: (b, c)),
        ],
        out_specs=pl.BlockSpec((nb * seq, tn), lambda b, c: (b, c)),
        out_shape=out_shape,
        compiler_params=_params("parallel", "parallel"),
        name="pos_dft_small",
    )(cmat, smat_neg, z16, a_s)


_HALO = SUBLANES


def _lru_kernel(rin_ref, sr_ref, cw_ref, cb_ref, wg_ref, bg_ref, cn_ref, h0_ref,
                y_ref, st_ref, xpad, a_f, u_f, a_b, u_b, *, seq, tc, chunk, cs):
    heads = tc // LRU_BLOCK
    n_chunks = seq // chunk

    def rows(c):
        return pl.multiple_of(c * chunk, chunk)

    zero_halo = jnp.zeros((_HALO, tc), F32)
    xpad[0:_HALO, :] = zero_halo
    xpad[seq + _HALO:seq + 2 * _HALO, :] = zero_halo

    def copy_body(c, carry):
        r0 = rows(c)
        xpad[pl.ds(r0 + _HALO, chunk), :] = rin_ref[pl.ds(r0, chunk), :]
        return carry
    lax.fori_loop(0, n_chunks, copy_body, 0)

    ext = chunk + 2 * _HALO

    def gate_body(c, carry):
        r0 = rows(c)
        xe = xpad[pl.ds(r0, ext), :]
        inner = slice(_HALO, _HALO + chunk)
        xc = cb_ref[...] + pltpu.roll(xe, 1, 0)[inner] * cw_ref[0:1, :]
        xc = xc + xe[inner] * cw_ref[1:2, :]
        xc = xc + pltpu.roll(xe, ext - 1, 0)[inner] * cw_ref[2:3, :]
        xc = xc + pltpu.roll(xe, ext - 2, 0)[inner] * cw_ref[3:4, :]
        for hh in range(heads):
            cols = slice(hh * LRU_BLOCK, (hh + 1) * LRU_BLOCK)
            xh = xc[:, cols]
            g = jnp.dot(xh.astype(BF16), wg_ref[hh], preferred_element_type=F32) + bg_ref[hh]
            cn = cn_ref[hh]
            for d, (a_s, u_s) in enumerate(((a_f, u_f), (a_b, u_b))):
                base = 2 * d * LRU_BLOCK
                gate_a = _sigmoid(g[:, base:base + LRU_BLOCK])
                gate_x = _sigmoid(g[:, base + LRU_BLOCK:base + 2 * LRU_BLOCK])
                log_a = cn[:, d * LRU_BLOCK:(d + 1) * LRU_BLOCK] * gate_a
                th = jnp.tanh(log_a)
                a_s[pl.ds(r0, chunk), cols] = jnp.exp(log_a)
                u_s[pl.ds(r0, chunk), cols] = jnp.sqrt(-2.0 * th / (1.0 - th)) * (gate_x * xh)
        return carry
    lax.fori_loop(0, n_chunks, gate_body, 0)

    row = lax.broadcasted_iota(jnp.int32, (SUBLANES, cs), 0)
    n_blocks = seq // SUBLANES
    for c0 in range(0, tc, cs):
        cols = slice(c0, c0 + cs)

        def scan_body(k, carry, cols=cols):
            carry_f, carry_b = carry
            rf = pl.multiple_of(k * SUBLANES, SUBLANES)
            rb = pl.multiple_of(seq - SUBLANES - k * SUBLANES, SUBLANES)
            a = a_f[pl.ds(rf, SUBLANES), cols]
            b = u_f[pl.ds(rf, SUBLANES), cols]
            for s in (1, 2, 4):
                m = row >= s
                a_sh = jnp.where(m, pltpu.roll(a, s, 0), 1.0)
                b_sh = jnp.where(m, pltpu.roll(b, s, 0), 0.0)
                b = b + a * b_sh
                a = a * a_sh
            h_fwd = a * carry_f + b
            a_f[pl.ds(rf, SUBLANES), cols] = h_fwd
            carry_f = jnp.broadcast_to(h_fwd[SUBLANES - 1:SUBLANES, :], (SUBLANES, cs))

            a = a_b[pl.ds(rb, SUBLANES), cols]
            b = u_b[pl.ds(rb, SUBLANES), cols]
            for s in (1, 2, 4):
                m = row < SUBLANES - s
                a_sh = jnp.where(m, pltpu.roll(a, SUBLANES - s, 0), 1.0)
                b_sh = jnp.where(m, pltpu.roll(b, SUBLANES - s, 0), 0.0)
                b = b + a * b_sh
                a = a * a_sh
            h_bwd = a * carry_b + b
            a_b[pl.ds(rb, SUBLANES), cols] = h_bwd
            carry_b = jnp.broadcast_to(h_bwd[0:1, :], (SUBLANES, cs))
            return carry_f, carry_b

        init = (jnp.broadcast_to(h0_ref[0:1, cols], (SUBLANES, cs)),
                jnp.broadcast_to(h0_ref[1:2, cols], (SUBLANES, cs)))
        fin_f, fin_b = lax.fori_loop(0, n_blocks, scan_body, init)
        st_ref[0:1, cols] = fin_f[0:1, :]
        st_ref[1:2, cols] = fin_b[0:1, :]

    def out_body(c, carry):
        r = pl.ds(rows(c), chunk)
        y_ref[r, :] = ((a_f[r, :] + a_b[r, :]) * sr_ref[r, :].astype(F32)).astype(BF16)
        return carry
    lax.fori_loop(0, n_chunks, out_body, 0)


def _lru(rin, z16, conv_w, conv_b, wg, bg, cn, h0, batch, seq, tc):
    n = batch * seq
    heads = tc // LRU_BLOCK
    chunk = min(seq, 256)
    cs = min(tc, 512)
    sr_block0 = SLOT_SR * COL_BLOCK // tc
    kern = functools.partial(_lru_kernel, seq=seq, tc=tc, chunk=chunk, cs=cs)
    return pl.pallas_call(
        kern,
        grid=(batch, LRU_WIDTH // tc),
        in_specs=[
            pl.BlockSpec((seq, tc), lambda b, c: (b, c)),
            pl.BlockSpec((seq, tc), lambda b, c: (b, sr_block0 + c)),
            pl.BlockSpec((CONV_WIDTH, tc), lambda b, c: (0, c)),
            pl.BlockSpec((1, tc), lambda b, c: (0, c)),
            pl.BlockSpec((heads, LRU_BLOCK, 4 * LRU_BLOCK), lambda b, c: (c, 0, 0)),
            pl.BlockSpec((heads, 1, 4 * LRU_BLOCK), lambda b, c: (c, 0, 0)),
            pl.BlockSpec((heads, 1, 2 * LRU_BLOCK), lambda b, c: (c, 0, 0)),
            pl.BlockSpec((None, 2, tc), lambda b, c: (b, 0, c)),
        ],
        out_specs=[
            pl.BlockSpec((seq, tc), lambda b, c: (b, c)),
            pl.BlockSpec((None, 2, tc), lambda b, c: (b, 0, c)),
        ],
        out_shape=[
            jax.ShapeDtypeStruct((n, LRU_WIDTH), BF16),
            jax.ShapeDtypeStruct((batch, 2, LRU_WIDTH), F32),
        ],
        scratch_shapes=[
            pltpu.VMEM((seq + 2 * _HALO, tc), F32),
            pltpu.VMEM((seq, tc), F32),
            pltpu.VMEM((seq, tc), F32),
            pltpu.VMEM((seq, tc), F32),
            pltpu.VMEM((seq, tc), F32),
        ],
        compiler_params=_params("parallel", "parallel"),
        name="conv_rglru",
    )(rin, z16, conv_w, conv_b, wg, bg, cn, h0)


def _merge_kernel(f_ref, sf_ref, yr_ref, wm_ref, wf_ref, wr_ref, gf_ref, gr_ref, o_ref, yf_ref):
    @pl.when(pl.program_id(1) == 0)
    def _():
        mix = jnp.dot(f_ref[...], wm_ref[...], preferred_element_type=F32)
        yf_ref[...] = (mix * sf_ref[...].astype(F32)).astype(BF16)

    pf = jnp.dot(yf_ref[...], wf_ref[...], preferred_element_type=F32)
    pr = jnp.dot(yr_ref[...], wr_ref[...], preferred_element_type=F32)
    o_ref[...] = (gf_ref[...].astype(F32) * pf + gr_ref[...].astype(F32) * pr).astype(BF16)


def _merge(f, z16, yr, w_fmix, w_bf, w_br, tm):
    n = f.shape[0]
    tn = COL_BLOCK
    fw = FOURIER_WIDTH
    return pl.pallas_call(
        _merge_kernel,
        grid=(n // tm, D_MODEL // tn),
        in_specs=[
            pl.BlockSpec((tm, fw), lambda i, j: (i, 0)),
            pl.BlockSpec((tm, fw), lambda i, j: (i, SLOT_SF * COL_BLOCK // fw)),
            pl.BlockSpec((tm, LRU_WIDTH), lambda i, j: (i, 0)),
            pl.BlockSpec((fw, fw), lambda i, j: (0, 0)),
            pl.BlockSpec((fw, tn), lambda i, j: (0, j)),
            pl.BlockSpec((LRU_WIDTH, tn), lambda i, j: (0, j)),
            pl.BlockSpec((tm, tn), lambda i, j: (i, SLOT_SGF + j)),
            pl.BlockSpec((tm, tn), lambda i, j: (i, SLOT_SGR + j)),
        ],
        out_specs=pl.BlockSpec((tm, tn), lambda i, j: (i, j)),
        out_shape=jax.ShapeDtypeStruct((n, D_MODEL), BF16),
        scratch_shapes=[pltpu.VMEM((tm, fw), BF16)],
        compiler_params=_params("parallel", "arbitrary"),
        name="branch_merge",
    )(f, z16, yr, w_fmix, w_bf, w_br, z16, z16)


def _outproj_kernel(m_ref, w_ref, x_ref, g_ref, o_ref):
    o_ref[...] = x_ref[...] + g_ref[...] * jnp.dot(m_ref[...], w_ref[...],
                                                    preferred_element_type=F32)


def _outproj(merged, w_out, x, gate, row_of, tm):
    n = x.shape[0]
    tn = COL_BLOCK
    return pl.pallas_call(
        _outproj_kernel,
        grid=(n // tm, D_MODEL // tn),
        in_specs=[
            pl.BlockSpec((tm, D_MODEL), lambda i, j: (i, 0)),
            pl.BlockSpec((D_MODEL, tn), lambda i, j: (0, j)),
            pl.BlockSpec((tm, tn), lambda i, j: (i, j)),
            pl.BlockSpec((None, 1, tn), lambda i, j: (row_of(i), 0, j)),
        ],
        out_specs=pl.BlockSpec((tm, tn), lambda i, j: (i, j)),
        out_shape=jax.ShapeDtypeStruct((n, D_MODEL), F32),
        compiler_params=_params("parallel", "parallel"),
        name="out_proj",
    )(merged, w_out, x, gate)


def _dft_mats(n):
    k = jnp.arange(n, dtype=jnp.int32)
    kt = (k[:, None] * k[None, :]) % n
    ang = kt.astype(F32) * (2.0 * math.pi / n)
    scale = 1.0 / math.sqrt(n)
    return jnp.cos(ang) * scale, jnp.sin(ang) * scale


def _per_head(v):
    return v.reshape(2, LRU_HEADS, LRU_BLOCK).transpose(1, 0, 2).reshape(LRU_HEADS, 1, 2 * LRU_BLOCK)


def _stream(x, batch, seq, row_of_token_tile, h0_layers, mods, weights, dft, tc):
    n = batch * seq
    x = x.reshape(n, D_MODEL)
    cmat, smat_neg, cs = dft
    states = []
    tm_norm, tm_in, tm_merge, tm_out = 512, 1024, 512, 1024
    for l in range(DEPTH):
        w = weights[l]
        shift, scale, gate = mods[l]
        h = _norm_mod(x, w["norm_g"], scale, shift, row_of_token_tile(tm_norm), tm_norm)
        z16, a_s, rin = _inproj(h, w["w_in"], cs, tm_in)
        f = _tdft(z16, a_s, cmat, smat_neg, batch, seq)
        yr, st = _lru(rin, z16, w["conv_w"], w["conv_b"], w["wg"], w["bg"], w["cn"],
                      h0_layers[l], batch, seq, tc)
        states.append(st)
        merged = _merge(f, z16, yr, w["w_fmix"], w["w_bf"], w["w_br"], tm_merge)
        x = _outproj(merged, w["w_out"], x, gate, row_of_token_tile(tm_out), tm_out)
    return x, states


def kernel(x_prompt, x_sample, state_lru, c, c_ctx, norm_g, w_ada, b_ada, w_in, w_fmix,
           conv_w, conv_b, w_gate_a, b_gate_a, w_gate_x, b_gate_x, lam,
           w_branch_f, w_branch_r, w_out, final_g):
    batch, seq, _ = x_prompt.shape
    dec_batch, dec_seq, _ = x_sample.shape
    assert 1 + dec_batch <= COND_ROWS

    cond = jnp.zeros((COND_ROWS, D_MODEL), F32)
    cond = cond.at[0].set(c_ctx).at[1:1 + dec_batch].set(c)
    mod = _ada(cond, w_ada, b_ada)
    mods = []
    for l in range(DEPTH):
        parts = jnp.split(mod[l], 3, axis=-1)
        mods.append(tuple(p.reshape(COND_ROWS, 1, D_MODEL) for p in parts))

    weights = []
    for l in range(DEPTH):
        wg = jnp.concatenate([w_gate_a[l, 0], w_gate_x[l, 0], w_gate_a[l, 1], w_gate_x[l, 1]],
                             axis=-1).astype(BF16)
        bg = jnp.stack([b_gate_a[l, 0], b_gate_x[l, 0], b_gate_a[l, 1], b_gate_x[l, 1]])
        bg = bg.reshape(4, LRU_HEADS, LRU_BLOCK).transpose(1, 0, 2).reshape(LRU_HEADS, 1, 4 * LRU_BLOCK)
        cn = _per_head(-LRU_C * jax.nn.softplus(-lam[l]))
        weights.append(dict(
            norm_g=norm_g[l].reshape(1, D_MODEL),
            w_in=w_in[l].astype(BF16),
            w_fmix=w_fmix[l].astype(BF16),
            conv_w=conv_w[l],
            conv_b=conv_b[l].reshape(1, LRU_WIDTH),
            wg=wg, bg=bg, cn=cn,
            w_bf=w_branch_f[l].astype(BF16),
            w_br=w_branch_r[l].astype(BF16),
            w_out=w_out[l].astype(BF16),
        ))

    cc, sc = _dft_mats(FOURIER_GROUP_CH)
    cs = jnp.concatenate([cc, sc], axis=1).astype(BF16)

    def dft_for(t):
        ct, st = _dft_mats(t)
        return ct.astype(BF16), (-st).astype(BF16), cs

    zeros_h0 = jnp.zeros((batch, 2, LRU_WIDTH), F32)
    x_ctx, ctx_states = _stream(
        x_prompt, batch, seq, lambda tm: (lambda i: 0), [zeros_h0] * DEPTH,
        mods, weights, dft_for(seq), tc=LRU_WIDTH)
    y_prompt = _final_norm(x_ctx, final_g.reshape(1, D_MODEL), 512).reshape(batch, seq, D_MODEL)
    new_state = jnp.stack(ctx_states, axis=1)

    x_lat, _ = _stream(
        x_sample, dec_batch, dec_seq, lambda tm: (lambda i: 1 + (i * tm) // dec_seq),
        [state_lru[:, l] for l in range(DEPTH)], mods, weights, dft_for(dec_seq), tc=256)
    y_sample = _final_norm(x_lat, final_g.reshape(1, D_MODEL), 512).reshape(dec_batch, dec_seq, D_MODEL)
    return (y_prompt, y_sample, new_state)
```

```python
import functools
import math
from typing import NamedTuple, Optional

import jax
import jax.numpy as jnp
from jax import lax
from jax.experimental import pallas as pl
from jax.experimental.pallas import tpu as pltpu

F32 = jnp.float32
BF16 = jnp.bfloat16

D_MODEL = 4096
DEPTH = 2
FOURIER_WIDTH = D_MODEL // 2
FOURIER_GROUP_CH = 256
LRU_WIDTH = D_MODEL // 2
LRU_BLOCK = 128
LRU_HEADS = LRU_WIDTH // LRU_BLOCK
CONV_WIDTH = 4
LRU_C = 8.0
IN_WIDTH = 2 * FOURIER_WIDTH + 2 * LRU_WIDTH + 2 * D_MODEL
EPS = 1e-6

SUBLANES = 8
LANES = 128
VMEM_LIMIT_BYTES = 58 * 1024 * 1024

COND_ROWS = 16
COL_BLOCK = 1024
SLOT_AC, SLOT_SF, SLOT_SR, SLOT_SGF, SLOT_SGR = 0, 2, 4, 6, 10
PACKED_BLOCKS = 14


def _sigmoid(x):
    return 0.5 * jnp.tanh(0.5 * x) + 0.5


def _params(*sem):
    return pltpu.CompilerParams(dimension_semantics=sem, vmem_limit_bytes=VMEM_LIMIT_BYTES)


def _ada_kernel(c_ref, w_ref, b_ref, o_ref):
    c = c_ref[...]
    s = (c * _sigmoid(c)).astype(BF16)
    w = w_ref[...].astype(BF16)
    o_ref[...] = jnp.dot(s, w, preferred_element_type=F32) + b_ref[...]


def _ada(cond, w_ada, b_ada):
    tn = 512
    n_out = 3 * D_MODEL
    return pl.pallas_call(
        _ada_kernel,
        grid=(DEPTH, n_out // tn),
        in_specs=[
            pl.BlockSpec((COND_ROWS, D_MODEL), lambda l, j: (0, 0)),
            pl.BlockSpec((None, D_MODEL, tn), lambda l, j: (l, 0, j)),
            pl.BlockSpec((None, 1, tn), lambda l, j: (l, 0, j)),
        ],
        out_specs=pl.BlockSpec((None, COND_ROWS, tn), lambda l, j: (l, 0, j)),
        out_shape=jax.ShapeDtypeStruct((DEPTH, COND_ROWS, n_out), F32),
        compiler_params=_params("parallel", "parallel"),
        name="ada_mod",
    )(cond, w_ada, b_ada.reshape(DEPTH, 1, n_out))


def _norm_mod_kernel(x_ref, g_ref, sc_ref, sh_ref, o_ref):
    x = x_ref[...]
    ms = jnp.mean(x * x, axis=-1, keepdims=True)
    y = (x * lax.rsqrt(ms + EPS)) * g_ref[...]
    o_ref[...] = (y * (1.0 + sc_ref[...]) + sh_ref[...]).astype(o_ref.dtype)


def _norm_kernel(x_ref, g_ref, o_ref):
    x = x_ref[...]
    ms = jnp.mean(x * x, axis=-1, keepdims=True)
    o_ref[...] = ((x * lax.rsqrt(ms + EPS)) * g_ref[...]).astype(o_ref.dtype)


def _norm_mod(x, g, scale, shift, row_of, tm):
    n = x.shape[0]
    return pl.pallas_call(
        _norm_mod_kernel,
        grid=(n // tm,),
        in_specs=[
            pl.BlockSpec((tm, D_MODEL), lambda i: (i, 0)),
            pl.BlockSpec((1, D_MODEL), lambda i: (0, 0)),
            pl.BlockSpec((None, 1, D_MODEL), lambda i: (row_of(i), 0, 0)),
            pl.BlockSpec((None, 1, D_MODEL), lambda i: (row_of(i), 0, 0)),
        ],
        out_specs=pl.BlockSpec((tm, D_MODEL), lambda i: (i, 0)),
        out_shape=jax.ShapeDtypeStruct((n, D_MODEL), BF16),
        compiler_params=_params("parallel"),
        name="norm_mod",
    )(x, g, scale, shift)


def _final_norm(x, g, tm):
    n = x.shape[0]
    return pl.pallas_call(
        _norm_kernel,
        grid=(n // tm,),
        in_specs=[
            pl.BlockSpec((tm, D_MODEL), lambda i: (i, 0)),
            pl.BlockSpec((1, D_MODEL), lambda i: (0, 0)),
        ],
        out_specs=pl.BlockSpec((tm, D_MODEL), lambda i: (i, 0)),
        out_shape=jax.ShapeDtypeStruct((n, D_MODEL), F32),
        compiler_params=_params("parallel"),
        name="final_norm",
    )(x, g)


def _inproj_kernel(h_ref, w_ref, cs_ref, z16_ref, as_ref, rin_ref):
    j = pl.program_id(1)

    def z():
        return jnp.dot(h_ref[...], w_ref[...], preferred_element_type=F32)

    @pl.when(j < 2)
    def _():
        zb = z().astype(BF16)
        for g in range(COL_BLOCK // FOURIER_GROUP_CH):
            cols = slice(g * FOURIER_GROUP_CH, (g + 1) * FOURIER_GROUP_CH)
            t = jnp.dot(zb[:, cols], cs_ref[...], preferred_element_type=F32)
            z16_ref[:, cols] = t[:, :FOURIER_GROUP_CH].astype(BF16)
            as_ref[:, cols] = t[:, FOURIER_GROUP_CH:].astype(BF16)

    @pl.when(((j >= 2) & (j < 4)) | ((j >= 6) & (j < 8)))
    def _():
        v = z()
        z16_ref[...] = (v * _sigmoid(v)).astype(BF16)

    @pl.when((j >= 4) & (j < 6))
    def _():
        rin_ref[...] = z()

    @pl.when(j >= 8)
    def _():
        z16_ref[...] = _sigmoid(z()).astype(BF16)


def _packed_slot(j):
    return jnp.where(j < 4, j, jnp.where(j < 6, 3, j - 2))


def _inproj(h, w_in, cs, tm):
    n = h.shape[0]
    tn = COL_BLOCK
    return pl.pallas_call(
        _inproj_kernel,
        grid=(n // tm, IN_WIDTH // tn),
        in_specs=[
            pl.BlockSpec((tm, D_MODEL), lambda i, j: (i, 0)),
            pl.BlockSpec((D_MODEL, tn), lambda i, j: (0, j)),
            pl.BlockSpec((FOURIER_GROUP_CH, 2 * FOURIER_GROUP_CH), lambda i, j: (0, 0)),
        ],
        out_specs=[
            pl.BlockSpec((tm, tn), lambda i, j: (i, _packed_slot(j))),
            pl.BlockSpec((tm, tn), lambda i, j: (i, jnp.minimum(j, 1))),
            pl.BlockSpec((tm, tn), lambda i, j: (i, jnp.clip(j - 4, 0, 1))),
        ],
        out_shape=[
            jax.ShapeDtypeStruct((n, PACKED_BLOCKS * COL_BLOCK), BF16),
            jax.ShapeDtypeStruct((n, FOURIER_WIDTH), BF16),
            jax.ShapeDtypeStruct((n, LRU_WIDTH), F32),
        ],
        compiler_params=_params("parallel", "arbitrary"),
        name="in_proj",
    )(h, w_in, cs)


_MIRROR_PAD = 16


def _tdft_sym_kernel(c_ref, s_ref, ac_ref, as_ref, sel_ref, lo_ref, hi_ref, *, tm):
    p = jnp.dot(c_ref[...], ac_ref[...], preferred_element_type=F32)
    q = jnp.dot(s_ref[...], as_ref[...], preferred_element_type=F32)
    lo_ref[...] = (p[:tm] + q[:tm]).astype(BF16)
    mirrored = (p - q).astype(BF16)
    hi_ref[...] = jnp.dot(sel_ref[...], mirrored, preferred_element_type=F32).astype(BF16)


def _tdft_sym_tables(seq, tm):
    half_tiles = seq // 2 // tm
    ext = tm + _MIRROR_PAD
    k = (jnp.arange(half_tiles, dtype=jnp.int32)[:, None] * tm
         + jnp.arange(ext, dtype=jnp.int32)[None, :]).reshape(-1)
    t = jnp.arange(seq, dtype=jnp.int32)
    ang = ((k[:, None] * t[None, :]) % seq).astype(F32) * (2.0 * math.pi / seq)
    scale = 1.0 / math.sqrt(seq)
    cmat = (jnp.cos(ang) * scale).astype(BF16).reshape(half_tiles, ext, seq)
    smat_neg = (jnp.sin(ang) * (-scale)).astype(BF16).reshape(half_tiles, ext, seq)
    r = jnp.arange(tm, dtype=jnp.int32)
    sel = (jnp.arange(ext, dtype=jnp.int32)[None, :] == (tm - r)[:, None]).astype(BF16)
    return cmat, smat_neg, sel


def _tdft_sym(z16, a_s, tables, batch, seq, tm):
    cmat, smat_neg, sel = tables
    tn = COL_BLOCK // 2
    ac_block0 = SLOT_AC * COL_BLOCK // tn
    half_tiles = seq // 2 // tm
    ext = tm + _MIRROR_PAD
    half = jax.ShapeDtypeStruct((batch * seq // 2, FOURIER_WIDTH), BF16)
    return pl.pallas_call(
        functools.partial(_tdft_sym_kernel, tm=tm),
        grid=(batch, FOURIER_WIDTH // tn, half_tiles),
        in_specs=[
            pl.BlockSpec((None, ext, seq), lambda b, c, i: (i, 0, 0)),
            pl.BlockSpec((None, ext, seq), lambda b, c, i: (i, 0, 0)),
            pl.BlockSpec((seq, tn), lambda b, c, i: (b, ac_block0 + c)),
            pl.BlockSpec((seq, tn), lambda b, c, i: (b, c)),
            pl.BlockSpec((tm, ext), lambda b, c, i: (0, 0)),
        ],
        out_specs=[
            pl.BlockSpec((tm, tn), lambda b, c, i: (b * half_tiles + i, c)),
            pl.BlockSpec((tm, tn), lambda b, c, i: (b * half_tiles + half_tiles - 1 - i, c)),
        ],
        out_shape=[half, half],
        compiler_params=_params("parallel", "parallel", "parallel"),
        name="pos_dft",
    )(cmat, smat_neg, z16, a_s, sel)


def _tdft_multi_kernel(c_ref, s_ref, ac_ref, as_ref, o_ref, *, nb, seq):
    for b in range(nb):
        r = slice(b * seq, (b + 1) * seq)
        o_ref[r, :] = (jnp.dot(c_ref[...], ac_ref[r, :], preferred_element_type=F32)
                       + jnp.dot(s_ref[...], as_ref[r, :], preferred_element_type=F32)).astype(BF16)


def _tdft_short_tables(seq):
    k = jnp.arange(seq, dtype=jnp.int32)
    ang = ((k[:, None] * k[None, :]) % seq).astype(F32) * (2.0 * math.pi / seq)
    scale = 1.0 / math.sqrt(seq)
    return (jnp.cos(ang) * scale).astype(BF16), (jnp.sin(ang) * (-scale)).astype(BF16)


def _tdft_short(z16, a_s, tables, batch, seq, nb):
    cmat, smat_neg = tables
    n = batch * seq
    tn = COL_BLOCK
    out_shape = jax.ShapeDtypeStruct((n, FOURIER_WIDTH), BF16)
    return pl.pallas_call(
        functools.partial(_tdft_multi_kernel, nb=nb, seq=seq),
        grid=(batch // nb, FOURIER_WIDTH // tn),
        in_specs=[
            pl.BlockSpec((seq, seq), lambda b, c: (0, 0)),
            pl.BlockSpec((seq, seq), lambda b, c: (0, 0)),
            pl.BlockSpec((nb * seq, tn), lambda b, c: (b, SLOT_AC + c)),
            pl.BlockSpec((nb * seq, tn), lambda b, c: (b, c)),
        ],
        out_specs=pl.BlockSpec((nb * seq, tn), lambda b, c: (b, c)),
        out_shape=out_shape,
        compiler_params=_params("parallel", "parallel"),
        name="pos_dft_small",
    )(cmat, smat_neg, z16, a_s)


_HALO = SUBLANES


def _lru_kernel(rin_ref, sr_ref, cw_ref, cb_ref, wg_ref, bg_ref, cn_ref, h0_ref,
                y_ref, st_ref, xpad, a_f, u_f, a_b, u_b, *, seq, tc, chunk, cs):
    heads = tc // LRU_BLOCK
    n_chunks = seq // chunk

    def rows(c):
        return pl.multiple_of(c * chunk, chunk)

    zero_halo = jnp.zeros((_HALO, tc), F32)
    xpad[0:_HALO, :] = zero_halo
    xpad[seq + _HALO:seq + 2 * _HALO, :] = zero_halo

    def copy_body(c, carry):
        r0 = rows(c)
        xpad[pl.ds(r0 + _HALO, chunk), :] = rin_ref[pl.ds(r0, chunk), :]
        return carry
    lax.fori_loop(0, n_chunks, copy_body, 0)

    ext = chunk + 2 * _HALO

    def gate_body(c, carry):
        r0 = rows(c)
        xe = xpad[pl.ds(r0, ext), :]
        inner = slice(_HALO, _HALO + chunk)
        xc = cb_ref[...] + pltpu.roll(xe, 1, 0)[inner] * cw_ref[0:1, :]
        xc = xc + xe[inner] * cw_ref[1:2, :]
        xc = xc + pltpu.roll(xe, ext - 1, 0)[inner] * cw_ref[2:3, :]
        xc = xc + pltpu.roll(xe, ext - 2, 0)[inner] * cw_ref[3:4, :]
        for hh in range(heads):
            cols = slice(hh * LRU_BLOCK, (hh + 1) * LRU_BLOCK)
            xh = xc[:, cols]
            g = jnp.dot(xh.astype(BF16), wg_ref[hh], preferred_element_type=F32) + bg_ref[hh]
            cn = cn_ref[hh]
            for d, (a_s, u_s) in enumerate(((a_f, u_f), (a_b, u_b))):
                base = 2 * d * LRU_BLOCK
                gate_a = _sigmoid(g[:, base:base + LRU_BLOCK])
                gate_x = _sigmoid(g[:, base + LRU_BLOCK:base + 2 * LRU_BLOCK])
                th = jnp.tanh(cn[:, d * LRU_BLOCK:(d + 1) * LRU_BLOCK] * gate_a)
                inv = 1.0 / (1.0 - th)
                a_s[pl.ds(r0, chunk), cols] = (1.0 + th) * inv
                u_s[pl.ds(r0, chunk), cols] = (2.0 * inv) * jnp.sqrt(-th) * (gate_x * xh)
        return carry
    lax.fori_loop(0, n_chunks, gate_body, 0)

    row = lax.broadcasted_iota(jnp.int32, (SUBLANES, cs), 0)
    n_blocks = seq // SUBLANES
    for c0 in range(0, tc, cs):
        cols = slice(c0, c0 + cs)

        def scan_body(k, carry, cols=cols):
            carry_f, carry_b = carry
            rf = pl.multiple_of(k * SUBLANES, SUBLANES)
            rb = pl.multiple_of(seq - SUBLANES - k * SUBLANES, SUBLANES)
            a = a_f[pl.ds(rf, SUBLANES), cols]
            b = u_f[pl.ds(rf, SUBLANES), cols]
            for s in (1, 2, 4):
                m = row >= s
                a_sh = jnp.where(m, pltpu.roll(a, s, 0), 1.0)
                b_sh = jnp.where(m, pltpu.roll(b, s, 0), 0.0)
                b = b + a * b_sh
                a = a * a_sh
            h_fwd = a * carry_f + b
            a_f[pl.ds(rf, SUBLANES), cols] = h_fwd
            carry_f = jnp.broadcast_to(h_fwd[SUBLANES - 1:SUBLANES, :], (SUBLANES, cs))

            a = a_b[pl.ds(rb, SUBLANES), cols]
            b = u_b[pl.ds(rb, SUBLANES), cols]
            for s in (1, 2, 4):
                m = row < SUBLANES - s
                a_sh = jnp.where(m, pltpu.roll(a, SUBLANES - s, 0), 1.0)
                b_sh = jnp.where(m, pltpu.roll(b, SUBLANES - s, 0), 0.0)
                b = b + a * b_sh
                a = a * a_sh
            h_bwd = a * carry_b + b
            a_b[pl.ds(rb, SUBLANES), cols] = h_bwd
            carry_b = jnp.broadcast_to(h_bwd[0:1, :], (SUBLANES, cs))
            return carry_f, carry_b

        init = (jnp.broadcast_to(h0_ref[0:1, cols], (SUBLANES, cs)),
                jnp.broadcast_to(h0_ref[1:2, cols], (SUBLANES, cs)))
        fin_f, fin_b = lax.fori_loop(0, n_blocks, scan_body, init)
        st_ref[0:1, cols] = fin_f[0:1, :]
        st_ref[1:2, cols] = fin_b[0:1, :]

    def out_body(c, carry):
        r = pl.ds(rows(c), chunk)
        y_ref[r, :] = ((a_f[r, :] + a_b[r, :]) * sr_ref[r, :].astype(F32)).astype(BF16)
        return carry
    lax.fori_loop(0, n_chunks, out_body, 0)


def _lru(rin, z16, conv_w, conv_b, wg, bg, cn, h0, batch, seq, tc):
    n = batch * seq
    heads = tc // LRU_BLOCK
    chunk = min(seq, 256)
    cs = min(tc, 512)
    sr_block0 = SLOT_SR * COL_BLOCK // tc
    kern = functools.partial(_lru_kernel, seq=seq, tc=tc, chunk=chunk, cs=cs)
    return pl.pallas_call(
        kern,
        grid=(batch, LRU_WIDTH // tc),
        in_specs=[
            pl.BlockSpec((seq, tc), lambda b, c: (b, c)),
            pl.BlockSpec((seq, tc), lambda b, c: (b, sr_block0 + c)),
            pl.BlockSpec((CONV_WIDTH, tc), lambda b, c: (0, c)),
            pl.BlockSpec((1, tc), lambda b, c: (0, c)),
            pl.BlockSpec((heads, LRU_BLOCK, 4 * LRU_BLOCK), lambda b, c: (c, 0, 0)),
            pl.BlockSpec((heads, 1, 4 * LRU_BLOCK), lambda b, c: (c, 0, 0)),
            pl.BlockSpec((heads, 1, 2 * LRU_BLOCK), lambda b, c: (c, 0, 0)),
            pl.BlockSpec((None, 2, tc), lambda b, c: (b, 0, c)),
        ],
        out_specs=[
            pl.BlockSpec((seq, tc), lambda b, c: (b, c)),
            pl.BlockSpec((None, 2, tc), lambda b, c: (b, 0, c)),
        ],
        out_shape=[
            jax.ShapeDtypeStruct((n, LRU_WIDTH), BF16),
            jax.ShapeDtypeStruct((batch, 2, LRU_WIDTH), F32),
        ],
        scratch_shapes=[
            pltpu.VMEM((seq + 2 * _HALO, tc), F32),
            pltpu.VMEM((seq, tc), F32),
            pltpu.VMEM((seq, tc), F32),
            pltpu.VMEM((seq, tc), F32),
            pltpu.VMEM((seq, tc), F32),
        ],
        compiler_params=_params("parallel", "parallel"),
        name="conv_rglru",
    )(rin, z16, conv_w, conv_b, wg, bg, cn, h0)


def _merge_kernel(flo_ref, fhi_ref, sf_ref, yr_ref, wm_ref, wf_ref, wr_ref, gf_ref, gr_ref,
                  o_ref, yf_ref, *, half_tiles):
    @pl.when(pl.program_id(1) == 0)
    def _():
        f = flo_ref[...]
        if half_tiles is not None:
            in_low_half = (pl.program_id(0) % (2 * half_tiles)) < half_tiles
            f = jnp.where(in_low_half, f, fhi_ref[...])
        mix = jnp.dot(f, wm_ref[...], preferred_element_type=F32)
        yf_ref[...] = (mix * sf_ref[...].astype(F32)).astype(BF16)

    pf = jnp.dot(yf_ref[...], wf_ref[...], preferred_element_type=F32)
    pr = jnp.dot(yr_ref[...], wr_ref[...], preferred_element_type=F32)
    o_ref[...] = (gf_ref[...].astype(F32) * pf + gr_ref[...].astype(F32) * pr).astype(BF16)


def _merge(f_lo, f_hi, z16, yr, w_fmix, w_bf, w_br, tm, half_tiles):
    n = yr.shape[0]
    tn = COL_BLOCK
    fw = FOURIER_WIDTH
    if half_tiles is None:
        lo_map = lambda i, j: (i, 0)
        hi_map = lambda i, j: (0, 0)
    else:
        def lo_map(i, j):
            b, r = i // (2 * half_tiles), i % (2 * half_tiles)
            return (b * half_tiles + jnp.minimum(r, half_tiles - 1), 0)

        def hi_map(i, j):
            b, r = i // (2 * half_tiles), i % (2 * half_tiles)
            return (b * half_tiles + jnp.maximum(r - half_tiles, 0), 0)
    return pl.pallas_call(
        functools.partial(_merge_kernel, half_tiles=half_tiles),
        grid=(n // tm, D_MODEL // tn),
        in_specs=[
            pl.BlockSpec((tm, fw), lo_map),
            pl.BlockSpec((tm, fw), hi_map),
            pl.BlockSpec((tm, fw), lambda i, j: (i, SLOT_SF * COL_BLOCK // fw)),
            pl.BlockSpec((tm, LRU_WIDTH), lambda i, j: (i, 0)),
            pl.BlockSpec((fw, fw), lambda i, j: (0, 0)),
            pl.BlockSpec((fw, tn), lambda i, j: (0, j)),
            pl.BlockSpec((LRU_WIDTH, tn), lambda i, j: (0, j)),
            pl.BlockSpec((tm, tn), lambda i, j: (i, SLOT_SGF + j)),
            pl.BlockSpec((tm, tn), lambda i, j: (i, SLOT_SGR + j)),
        ],
        out_specs=pl.BlockSpec((tm, tn), lambda i, j: (i, j)),
        out_shape=jax.ShapeDtypeStruct((n, D_MODEL), BF16),
        scratch_shapes=[pltpu.VMEM((tm, fw), BF16)],
        compiler_params=_params("parallel", "arbitrary"),
        name="branch_merge",
    )(f_lo, f_hi, z16, yr, w_fmix, w_bf, w_br, z16, z16)


def _outproj_kernel(m_ref, w_ref, x_ref, g_ref, o_ref):
    o_ref[...] = x_ref[...] + g_ref[...] * jnp.dot(m_ref[...], w_ref[...],
                                                    preferred_element_type=F32)


def _outproj(merged, w_out, x, gate, row_of, tm):
    n = x.shape[0]
    tn = COL_BLOCK
    return pl.pallas_call(
        _outproj_kernel,
        grid=(n // tm, D_MODEL // tn),
        in_specs=[
            pl.BlockSpec((tm, D_MODEL), lambda i, j: (i, 0)),
            pl.BlockSpec((D_MODEL, tn), lambda i, j: (0, j)),
            pl.BlockSpec((tm, tn), lambda i, j: (i, j)),
            pl.BlockSpec((None, 1, tn), lambda i, j: (row_of(i), 0, j)),
        ],
        out_specs=pl.BlockSpec((tm, tn), lambda i, j: (i, j)),
        out_shape=jax.ShapeDtypeStruct((n, D_MODEL), F32),
        compiler_params=_params("parallel", "parallel"),
        name="out_proj",
    )(merged, w_out, x, gate)


class _Tiles(NamedTuple):
    norm: int
    inproj: int
    merge: int
    outproj: int
    lru_channels: int
    dft_rows: Optional[int]
    dft_batch: Optional[int]


_LONG_SEQ = 1024


def _tiles(seq):
    if seq >= _LONG_SEQ:
        return _Tiles(norm=512, inproj=1024, merge=512, outproj=1024, lru_channels=256,
                      dft_rows=512, dft_batch=None)
    return _Tiles(norm=512, inproj=1024, merge=512, outproj=1024, lru_channels=LRU_WIDTH,
                  dft_rows=None, dft_batch=8)


def _channel_dft_table():
    k = jnp.arange(FOURIER_GROUP_CH, dtype=jnp.int32)
    ang = ((k[:, None] * k[None, :]) % FOURIER_GROUP_CH).astype(F32) * (2.0 * math.pi / FOURIER_GROUP_CH)
    scale = 1.0 / math.sqrt(FOURIER_GROUP_CH)
    return jnp.concatenate([jnp.cos(ang) * scale, jnp.sin(ang) * scale], axis=1).astype(BF16)


def _per_head(v):
    return v.reshape(2, LRU_HEADS, LRU_BLOCK).transpose(1, 0, 2).reshape(LRU_HEADS, 1, 2 * LRU_BLOCK)


def _stream(x, batch, seq, row_of_token_tile, h0_layers, mods, weights, cs, final_g):
    n = batch * seq
    x = x.reshape(n, D_MODEL)
    tiles = _tiles(seq)
    if tiles.dft_rows is not None:
        dft_tables = _tdft_sym_tables(seq, tiles.dft_rows)
        half_tiles = seq // 2 // tiles.merge
    else:
        dft_tables = _tdft_short_tables(seq)
        half_tiles = None
    states = []
    for l in range(DEPTH):
        w = weights[l]
        shift, scale, gate = mods[l]
        h = _norm_mod(x, w["norm_g"], scale, shift, row_of_token_tile(tiles.norm), tiles.norm)
        z16, a_s, rin = _inproj(h, w["w_in"], cs, tiles.inproj)
        if tiles.dft_rows is not None:
            f_lo, f_hi = _tdft_sym(z16, a_s, dft_tables, batch, seq, tiles.dft_rows)
        else:
            f_lo = f_hi = _tdft_short(z16, a_s, dft_tables, batch, seq, tiles.dft_batch)
        yr, st = _lru(rin, z16, w["conv_w"], w["conv_b"], w["wg"], w["bg"], w["cn"],
                      h0_layers[l], batch, seq, tiles.lru_channels)
        states.append(st)
        merged = _merge(f_lo, f_hi, z16, yr, w["w_fmix"], w["w_bf"], w["w_br"], tiles.merge, half_tiles)
        x = _outproj(merged, w["w_out"], x, gate, row_of_token_tile(tiles.outproj), tiles.outproj)
    y = _final_norm(x, final_g, tiles.norm).reshape(batch, seq, D_MODEL)
    return y, states


def kernel(x_prompt, x_sample, state_lru, c, c_ctx, norm_g, w_ada, b_ada, w_in, w_fmix,
           conv_w, conv_b, w_gate_a, b_gate_a, w_gate_x, b_gate_x, lam,
           w_branch_f, w_branch_r, w_out, final_g):
    batch, seq, _ = x_prompt.shape
    dec_batch, dec_seq, _ = x_sample.shape
    assert 1 + dec_batch <= COND_ROWS

    cond = jnp.zeros((COND_ROWS, D_MODEL), F32)
    cond = cond.at[0].set(c_ctx).at[1:1 + dec_batch].set(c)
    mod = _ada(cond, w_ada, b_ada)
    mods = []
    for l in range(DEPTH):
        parts = jnp.split(mod[l], 3, axis=-1)
        mods.append(tuple(p.reshape(COND_ROWS, 1, D_MODEL) for p in parts))

    weights = []
    for l in range(DEPTH):
        wg = jnp.concatenate([w_gate_a[l, 0], w_gate_x[l, 0], w_gate_a[l, 1], w_gate_x[l, 1]],
                             axis=-1).astype(BF16)
        bg = jnp.stack([b_gate_a[l, 0], b_gate_x[l, 0], b_gate_a[l, 1], b_gate_x[l, 1]])
        bg = bg.reshape(4, LRU_HEADS, LRU_BLOCK).transpose(1, 0, 2).reshape(LRU_HEADS, 1, 4 * LRU_BLOCK)
        cn = _per_head((-0.5 * LRU_C) * jax.nn.softplus(-lam[l]))
        weights.append(dict(
            norm_g=norm_g[l].reshape(1, D_MODEL),
            w_in=w_in[l].astype(BF16),
            w_fmix=w_fmix[l].astype(BF16),
            conv_w=conv_w[l],
            conv_b=conv_b[l].reshape(1, LRU_WIDTH),
            wg=wg, bg=bg, cn=cn,
            w_bf=w_branch_f[l].astype(BF16),
            w_br=w_branch_r[l].astype(BF16),
            w_out=w_out[l].astype(BF16),
        ))

    cs = _channel_dft_table()
    final_g = final_g.reshape(1, D_MODEL)

    zeros_h0 = jnp.zeros((batch, 2, LRU_WIDTH), F32)
    y_prompt, ctx_states = _stream(
        x_prompt, batch, seq, lambda tm: (lambda i: 0), [zeros_h0] * DEPTH,
        mods, weights, cs, final_g)
    new_state = jnp.stack(ctx_states, axis=1)

    y_sample, _ = _stream(
        x_sample, dec_batch, dec_seq, lambda tm: (lambda i: 1 + (i * tm) // dec_seq),
        [state_lru[:, l] for l in range(DEPTH)], mods, weights, cs, final_g)
    return (y_prompt, y_sample, new_state)
```

```python
import functools
import math
from typing import NamedTuple, Optional

import jax
import jax.numpy as jnp
from jax import lax
from jax.experimental import pallas as pl
from jax.experimental.pallas import tpu as pltpu

F32 = jnp.float32
BF16 = jnp.bfloat16

D_MODEL = 4096
DEPTH = 2
FOURIER_WIDTH = D_MODEL // 2
FOURIER_GROUP_CH = 256
LRU_WIDTH = D_MODEL // 2
LRU_BLOCK = 128
LRU_HEADS = LRU_WIDTH // LRU_BLOCK
CONV_WIDTH = 4
LRU_C = 8.0
IN_WIDTH = 2 * FOURIER_WIDTH + 2 * LRU_WIDTH + 2 * D_MODEL
EPS = 1e-6

SUBLANES = 8
LANES = 128
VMEM_LIMIT_BYTES = 58 * 1024 * 1024
F32_MIN_NORMAL = 2.0 ** -126

COND_ROWS = 16
COL_BLOCK = 1024
SLOT_AC, SLOT_SF, SLOT_SR, SLOT_SGF, SLOT_SGR = 0, 2, 4, 6, 10
PACKED_BLOCKS = 14


def _sigmoid(x):
    return 0.5 * jnp.tanh(0.5 * x) + 0.5


def _params(*sem):
    return pltpu.CompilerParams(dimension_semantics=sem, vmem_limit_bytes=VMEM_LIMIT_BYTES)


def _ada_kernel(c_ref, w_ref, b_ref, o_ref):
    c = c_ref[...]
    s = (c * _sigmoid(c)).astype(BF16)
    w = w_ref[...].astype(BF16)
    o_ref[...] = jnp.dot(s, w, preferred_element_type=F32) + b_ref[...]


def _ada(cond, w_ada, b_ada):
    tn = 512
    n_out = 3 * D_MODEL
    return pl.pallas_call(
        _ada_kernel,
        grid=(DEPTH, n_out // tn),
        in_specs=[
            pl.BlockSpec((COND_ROWS, D_MODEL), lambda l, j: (0, 0)),
            pl.BlockSpec((None, D_MODEL, tn), lambda l, j: (l, 0, j)),
            pl.BlockSpec((None, 1, tn), lambda l, j: (l, 0, j)),
        ],
        out_specs=pl.BlockSpec((None, COND_ROWS, tn), lambda l, j: (l, 0, j)),
        out_shape=jax.ShapeDtypeStruct((DEPTH, COND_ROWS, n_out), F32),
        compiler_params=_params("parallel", "parallel"),
        name="ada_mod",
    )(cond, w_ada, b_ada.reshape(DEPTH, 1, n_out))


def _norm_mod_kernel(x_ref, g_ref, sc_ref, sh_ref, o_ref):
    x = x_ref[...]
    ms = jnp.mean(x * x, axis=-1, keepdims=True)
    y = (x * lax.rsqrt(ms + EPS)) * g_ref[...]
    o_ref[...] = (y * (1.0 + sc_ref[...]) + sh_ref[...]).astype(o_ref.dtype)


def _norm_mod(x, g, scale, shift, row_of, tm):
    n = x.shape[0]
    return pl.pallas_call(
        _norm_mod_kernel,
        grid=(n // tm,),
        in_specs=[
            pl.BlockSpec((tm, D_MODEL), lambda i: (i, 0)),
            pl.BlockSpec((1, D_MODEL), lambda i: (0, 0)),
            pl.BlockSpec((None, 1, D_MODEL), lambda i: (row_of(i), 0, 0)),
            pl.BlockSpec((None, 1, D_MODEL), lambda i: (row_of(i), 0, 0)),
        ],
        out_specs=pl.BlockSpec((tm, D_MODEL), lambda i: (i, 0)),
        out_shape=jax.ShapeDtypeStruct((n, D_MODEL), BF16),
        compiler_params=_params("parallel"),
        name="norm_mod",
    )(x, g, scale, shift)


def _inproj_kernel(h_ref, w_ref, cs_ref, z16_ref, as_ref, rin_ref):
    j = pl.program_id(1)

    def z():
        return jnp.dot(h_ref[...], w_ref[...], preferred_element_type=F32)

    @pl.when(j < 2)
    def _():
        zb = z().astype(BF16)
        for g in range(COL_BLOCK // FOURIER_GROUP_CH):
            cols = slice(g * FOURIER_GROUP_CH, (g + 1) * FOURIER_GROUP_CH)
            t = jnp.dot(zb[:, cols], cs_ref[...], preferred_element_type=F32)
            z16_ref[:, cols] = t[:, :FOURIER_GROUP_CH].astype(BF16)
            as_ref[:, cols] = t[:, FOURIER_GROUP_CH:].astype(BF16)

    @pl.when(((j >= 2) & (j < 4)) | ((j >= 6) & (j < 8)))
    def _():
        v = z()
        z16_ref[...] = (v * _sigmoid(v)).astype(BF16)

    @pl.when((j >= 4) & (j < 6))
    def _():
        rin_ref[...] = z()

    @pl.when(j >= 8)
    def _():
        z16_ref[...] = _sigmoid(z()).astype(BF16)


def _packed_slot(j):
    return jnp.where(j < 4, j, jnp.where(j < 6, 3, j - 2))


def _inproj(h, w_in, layer, cs, tm):
    n = h.shape[0]
    tn = COL_BLOCK
    return pl.pallas_call(
        _inproj_kernel,
        grid=(n // tm, IN_WIDTH // tn),
        in_specs=[
            pl.BlockSpec((tm, D_MODEL), lambda i, j: (i, 0)),
            pl.BlockSpec((None, D_MODEL, tn), lambda i, j: (layer, 0, j)),
            pl.BlockSpec((FOURIER_GROUP_CH, 2 * FOURIER_GROUP_CH), lambda i, j: (0, 0)),
        ],
        out_specs=[
            pl.BlockSpec((tm, tn), lambda i, j: (i, _packed_slot(j))),
            pl.BlockSpec((tm, tn), lambda i, j: (i, jnp.minimum(j, 1))),
            pl.BlockSpec((tm, tn), lambda i, j: (i, jnp.clip(j - 4, 0, 1))),
        ],
        out_shape=[
            jax.ShapeDtypeStruct((n, PACKED_BLOCKS * COL_BLOCK), BF16),
            jax.ShapeDtypeStruct((n, FOURIER_WIDTH), BF16),
            jax.ShapeDtypeStruct((n, LRU_WIDTH), F32),
        ],
        compiler_params=_params("parallel", "arbitrary"),
        name="in_proj",
    )(h, w_in, cs)


_HALO = SUBLANES
_GATE_ROWS = 256
_SEGMENTS = SUBLANES
_SEG_PAD = SUBLANES


def _conv_taps(xe, cw_ref, cb_ref, rows):
    ext = rows + 2 * _HALO
    inner = slice(_HALO, _HALO + rows)
    xc = cb_ref[...] + pltpu.roll(xe, 1, 0)[inner] * cw_ref[0:1, :]
    xc = xc + xe[inner] * cw_ref[1:2, :]
    xc = xc + pltpu.roll(xe, ext - 1, 0)[inner] * cw_ref[2:3, :]
    return xc + pltpu.roll(xe, ext - 2, 0)[inner] * cw_ref[3:4, :]


def _gate_terms(gh, xh, cq):
    out = []
    for d in range(2):
        base = 2 * d * LRU_BLOCK
        t_a = jnp.tanh(gh[:, base:base + LRU_BLOCK])
        t_x = jnp.tanh(gh[:, base + LRU_BLOCK:base + 2 * LRU_BLOCK])
        c = cq[:, d * LRU_BLOCK:(d + 1) * LRU_BLOCK]
        s = jnp.tanh(c * t_a + c)
        inv = 1.0 / (1.0 + s)
        root = s * lax.rsqrt(jnp.maximum(s, F32_MIN_NORMAL))
        a = (1.0 - s) * inv
        u = (inv * root) * (xh * t_x + xh)
        out.append((a, u))
    return out


def _block_scan(a, b, row, reverse):
    for s in (1, 2, 4):
        if reverse:
            m, sh = row < SUBLANES - s, SUBLANES - s
        else:
            m, sh = row >= s, s
        a_sh = jnp.where(m, pltpu.roll(a, sh, 0), 1.0)
        b_sh = jnp.where(m, pltpu.roll(b, sh, 0), 0.0)
        b = b + a * b_sh
        a = a * a_sh
    return a, b


_SCAN_ROWS = 2 * SUBLANES


def _scan_both(a_f, u_f, a_b, u_b, h_f, h_b, h0_ref, st_ref, *, seq, tc, cs):
    row = lax.broadcasted_iota(jnp.int32, (SUBLANES, cs), 0)
    for c0 in range(0, tc, cs):
        cols = slice(c0, c0 + cs)

        def body(k, carry, cols=cols):
            carry_f, carry_b = carry
            rf = pl.multiple_of(k * _SCAN_ROWS, _SCAN_ROWS)
            rb = pl.multiple_of(seq - _SCAN_ROWS - k * _SCAN_ROWS, _SCAN_ROWS)
            a2 = a_f[pl.ds(rf, _SCAN_ROWS), cols]
            b2 = u_f[pl.ds(rf, _SCAN_ROWS), cols].astype(F32)
            hs = []
            for p in (0, 1):
                blk = slice(p * SUBLANES, (p + 1) * SUBLANES)
                a, b = _block_scan(a2[blk], b2[blk], row, False)
                h = a * carry_f + b
                hs.append(h)
                carry_f = jnp.broadcast_to(h[SUBLANES - 1:SUBLANES, :], (SUBLANES, cs))
            h_f[pl.ds(rf, _SCAN_ROWS), cols] = jnp.concatenate(hs, axis=0)

            a2 = a_b[pl.ds(rb, _SCAN_ROWS), cols]
            b2 = u_b[pl.ds(rb, _SCAN_ROWS), cols].astype(F32)
            hs = [None, None]
            for p in (1, 0):
                blk = slice(p * SUBLANES, (p + 1) * SUBLANES)
                a, b = _block_scan(a2[blk], b2[blk], row, True)
                h = a * carry_b + b
                hs[p] = h
                carry_b = jnp.broadcast_to(h[0:1, :], (SUBLANES, cs))
            h_b[pl.ds(rb, _SCAN_ROWS), cols] = jnp.concatenate(hs, axis=0)
            return carry_f, carry_b

        init = (jnp.broadcast_to(h0_ref[0:1, cols], (SUBLANES, cs)),
                jnp.broadcast_to(h0_ref[1:2, cols], (SUBLANES, cs)))
        fin_f, fin_b = lax.fori_loop(0, seq // _SCAN_ROWS, body, init)
        st_ref[0:1, cols] = fin_f[0:1, :]
        st_ref[1:2, cols] = fin_b[0:1, :]


def _write_gated_sum(h_f, h_b, sr_ref, y_ref, seq):
    def body(c, carry):
        r = pl.ds(pl.multiple_of(c * _GATE_ROWS, _GATE_ROWS), _GATE_ROWS)
        y_ref[r, :] = ((h_f[r, :] + h_b[r, :]) * sr_ref[r, :].astype(F32)).astype(BF16)
        return carry
    lax.fori_loop(0, seq // _GATE_ROWS, body, 0)


def _lru_kernel(rin_ref, sr_ref, cw_ref, cb_ref, wg_ref, bg_ref, cq_ref, h0_ref,
                y_ref, st_ref, xpad, a_f, u_f, a_b, u_b, *, seq, tc, cs):
    heads = tc // LRU_BLOCK
    n_chunks = seq // _GATE_ROWS

    def rows(c):
        return pl.multiple_of(c * _GATE_ROWS, _GATE_ROWS)

    zero_halo = jnp.zeros((_HALO, tc), F32)
    xpad[0:_HALO, :] = zero_halo
    xpad[seq + _HALO:seq + 2 * _HALO, :] = zero_halo

    def copy_body(c, carry):
        r0 = rows(c)
        xpad[pl.ds(r0 + _HALO, _GATE_ROWS), :] = rin_ref[pl.ds(r0, _GATE_ROWS), :]
        return carry
    lax.fori_loop(0, n_chunks, copy_body, 0)

    def gate_body(c, carry):
        r0 = rows(c)
        xc = _conv_taps(xpad[pl.ds(r0, _GATE_ROWS + 2 * _HALO), :], cw_ref, cb_ref, _GATE_ROWS)
        for hh in range(heads):
            cols = slice(hh * LRU_BLOCK, (hh + 1) * LRU_BLOCK)
            xh = xc[:, cols]
            gh = jnp.dot(xh.astype(BF16), wg_ref[hh], preferred_element_type=F32) + bg_ref[hh]
            (af, uf), (ab, ub) = _gate_terms(gh, xh, cq_ref[hh])
            a_f[pl.ds(r0, _GATE_ROWS), cols] = af
            u_f[pl.ds(r0, _GATE_ROWS), cols] = uf
            a_b[pl.ds(r0, _GATE_ROWS), cols] = ab
            u_b[pl.ds(r0, _GATE_ROWS), cols] = ub
        return carry
    lax.fori_loop(0, n_chunks, gate_body, 0)

    _scan_both(a_f, u_f, a_b, u_b, a_f, a_b, h0_ref, st_ref, seq=seq, tc=tc, cs=cs)
    _write_gated_sum(a_f, a_b, sr_ref, y_ref, seq)


def _lru(rin, z16, conv_w, conv_b, wg, bg, cq, h0, batch, seq, tc):
    n = batch * seq
    heads = tc // LRU_BLOCK
    cs = min(tc, 512)
    sr_block0 = SLOT_SR * COL_BLOCK // tc
    kern = functools.partial(_lru_kernel, seq=seq, tc=tc, cs=cs)
    return pl.pallas_call(
        kern,
        grid=(batch, LRU_WIDTH // tc),
        in_specs=[
            pl.BlockSpec((seq, tc), lambda b, c: (b, c)),
            pl.BlockSpec((seq, tc), lambda b, c: (b, sr_block0 + c)),
            pl.BlockSpec((CONV_WIDTH, tc), lambda b, c: (0, c)),
            pl.BlockSpec((1, tc), lambda b, c: (0, c)),
            pl.BlockSpec((heads, LRU_BLOCK, 4 * LRU_BLOCK), lambda b, c: (c, 0, 0)),
            pl.BlockSpec((heads, 1, 4 * LRU_BLOCK), lambda b, c: (c, 0, 0)),
            pl.BlockSpec((heads, 1, 2 * LRU_BLOCK), lambda b, c: (c, 0, 0)),
            pl.BlockSpec((None, 2, tc), lambda b, c: (b, 0, c)),
        ],
        out_specs=[
            pl.BlockSpec((seq, tc), lambda b, c: (b, c)),
            pl.BlockSpec((None, 2, tc), lambda b, c: (b, 0, c)),
        ],
        out_shape=[
            jax.ShapeDtypeStruct((n, LRU_WIDTH), BF16),
            jax.ShapeDtypeStruct((batch, 2, LRU_WIDTH), F32),
        ],
        scratch_shapes=[
            pltpu.VMEM((seq + 2 * _HALO, tc), F32),
            pltpu.VMEM((seq, tc), F32),
            pltpu.VMEM((seq, tc), F32),
            pltpu.VMEM((seq, tc), F32),
            pltpu.VMEM((seq, tc), F32),
        ],
        compiler_params=_params("parallel", "parallel"),
        name="conv_rglru",
    )(rin, z16, conv_w, conv_b, wg, bg, cq, h0)


_SCAN_UNROLL = 4


def _scan_kernel(af_ref, uf_ref, ab_ref, ub_ref, sr_ref, h0_ref, pm_ref, pmt_ref, y_ref, st_ref,
                 h_f, sr_seg, y_seg, *, seq, tc):
    seg = seq // _SEGMENTS
    per_chunk = _GATE_ROWS // _SEGMENTS
    n_chunks = seq // _GATE_ROWS
    row = lax.broadcasted_iota(jnp.int32, (_SEGMENTS, tc), 0)
    halves = (slice(0, SUBLANES), slice(SUBLANES, _SCAN_ROWS))

    def chunk_slabs(k):
        return [slice(s * seg + k * per_chunk, s * seg + (k + 1) * per_chunk)
                for s in range(_SEGMENTS)]

    for k in range(n_chunks):
        sr_pos = jnp.concatenate([sr_ref[sl, :] for sl in chunk_slabs(k)], axis=0)
        sr_seg[k * _GATE_ROWS:(k + 1) * _GATE_ROWS, :] = jnp.dot(
            pm_ref[...], sr_pos, preferred_element_type=F32).astype(BF16)

    def fwd_rows(i):
        return pl.ds(pl.multiple_of(i * _SCAN_ROWS, _SCAN_ROWS), _SCAN_ROWS)

    def bwd_rows(i):
        return pl.ds(pl.multiple_of(seq - _SCAN_ROWS - i * _SCAN_ROWS, _SCAN_ROWS), _SCAN_ROWS)

    def local_ends(i, carry):
        h_fwd, p_fwd, h_bwd, p_bwd = carry
        a2, u2 = af_ref[fwd_rows(i), :], uf_ref[fwd_rows(i), :].astype(F32)
        for blk in halves:
            h_fwd = a2[blk] * h_fwd + u2[blk]
            p_fwd = a2[blk] * p_fwd
        a2, u2 = ab_ref[bwd_rows(i), :], ub_ref[bwd_rows(i), :].astype(F32)
        for blk in reversed(halves):
            h_bwd = a2[blk] * h_bwd + u2[blk]
            p_bwd = a2[blk] * p_bwd
        return h_fwd, p_fwd, h_bwd, p_bwd

    zeros = jnp.zeros((_SEGMENTS, tc), F32)
    ones = jnp.ones((_SEGMENTS, tc), F32)
    end_f, prod_f, end_b, prod_b = lax.fori_loop(0, seq // _SCAN_ROWS, local_ends,
                                                 (zeros, ones, zeros, ones), unroll=_SCAN_UNROLL)

    h0_f = jnp.broadcast_to(h0_ref[0:1, :], (_SEGMENTS, tc))
    h0_b = jnp.broadcast_to(h0_ref[1:2, :], (_SEGMENTS, tc))
    a_cum, b_cum = _block_scan(prod_f, end_f, row, False)
    true_end_f = a_cum * h0_f + b_cum
    start_f = jnp.where(row >= 1, pltpu.roll(true_end_f, 1, 0), h0_f)
    a_cum, b_cum = _block_scan(prod_b, end_b, row, True)
    true_end_b = a_cum * h0_b + b_cum
    start_b = jnp.where(row < _SEGMENTS - 1, pltpu.roll(true_end_b, _SEGMENTS - 1, 0), h0_b)
    st_ref[0:1, :] = true_end_f[_SEGMENTS - 1:_SEGMENTS, :]
    st_ref[1:2, :] = true_end_b[0:1, :]

    def forward(i, h):
        a2, u2 = af_ref[fwd_rows(i), :], uf_ref[fwd_rows(i), :].astype(F32)
        hs = []
        for blk in halves:
            h = a2[blk] * h + u2[blk]
            hs.append(h)
        h_f[fwd_rows(i), :] = jnp.concatenate(hs, axis=0)
        return h
    lax.fori_loop(0, seq // _SCAN_ROWS, forward, start_f, unroll=_SCAN_UNROLL)

    def backward(i, h):
        a2, u2 = ab_ref[bwd_rows(i), :], ub_ref[bwd_rows(i), :].astype(F32)
        hf2, sr2 = h_f[bwd_rows(i), :], sr_seg[bwd_rows(i), :].astype(F32)
        ys = [None, None]
        for p in (1, 0):
            blk = halves[p]
            h = a2[blk] * h + u2[blk]
            ys[p] = (hf2[blk] + h) * sr2[blk]
        y_seg[bwd_rows(i), :] = jnp.concatenate(ys, axis=0).astype(BF16)
        return h
    lax.fori_loop(0, seq // _SCAN_ROWS, backward, start_b, unroll=_SCAN_UNROLL)

    for k in range(n_chunks):
        y_pos = jnp.dot(pmt_ref[...], y_seg[k * _GATE_ROWS:(k + 1) * _GATE_ROWS, :],
                        preferred_element_type=F32).astype(BF16)
        for s, sl in enumerate(chunk_slabs(k)):
            y_ref[sl, :] = y_pos[s * per_chunk:(s + 1) * per_chunk, :]


def _segment_permutation():
    per_chunk = _GATE_ROWS // _SEGMENTS
    r = jnp.arange(_GATE_ROWS, dtype=jnp.int32)
    src = (r % _SEGMENTS) * per_chunk + r // _SEGMENTS
    return (jnp.arange(_GATE_ROWS, dtype=jnp.int32)[None, :] == src[:, None]).astype(BF16)


def _scan(a_f, u_f, a_b, u_b, z16, h0, batch, seq, tc):
    n = batch * seq
    sr_block0 = SLOT_SR * COL_BLOCK // tc
    tile = pl.BlockSpec((seq, tc), lambda b, c: (b, c))
    perm = _segment_permutation()
    perm_spec = pl.BlockSpec((_GATE_ROWS, _GATE_ROWS), lambda b, c: (0, 0))
    return pl.pallas_call(
        functools.partial(_scan_kernel, seq=seq, tc=tc),
        grid=(batch, LRU_WIDTH // tc),
        in_specs=[
            tile, tile, tile, tile,
            pl.BlockSpec((seq, tc), lambda b, c: (b, sr_block0 + c)),
            pl.BlockSpec((None, 2, tc), lambda b, c: (b, 0, c)),
            perm_spec, perm_spec,
        ],
        out_specs=[
            tile,
            pl.BlockSpec((None, 2, tc), lambda b, c: (b, 0, c)),
        ],
        out_shape=[
            jax.ShapeDtypeStruct((n, LRU_WIDTH), BF16),
            jax.ShapeDtypeStruct((batch, 2, LRU_WIDTH), F32),
        ],
        scratch_shapes=[pltpu.VMEM((seq, tc), F32), pltpu.VMEM((seq, tc), BF16),
                        pltpu.VMEM((seq, tc), BF16)],
        compiler_params=_params("parallel", "parallel"),
        name="rglru_scan",
    )(a_f, u_f, a_b, u_b, z16, h0, perm, perm.T)


_MIRROR_PAD = 16


def _tdft_gates_kernel(c_ref, s_ref, ac_ref, as_ref, sel_ref, rin_ref, cw_ref, cb_ref, wg_ref,
                       bg_ref, cq_ref, lo_ref, hi_ref, af_ref, uf_ref, ab_ref, ub_ref, xs,
                       *, tm, seq):
    seg = seq // _SEGMENTS
    pitch = seg + _SEG_PAD
    for s in range(_SEGMENTS):
        xs[s * pitch:s * pitch + seg, :] = rin_ref[s * seg:(s + 1) * seg, :]
    row = lax.broadcasted_iota(jnp.int32, (_SEGMENTS, LRU_BLOCK), 0)

    def positions(j):
        if 0 <= j < seg:
            return xs[pl.ds(j, _SEGMENTS, stride=pitch), :]
        if j < 0:
            v = xs[pl.ds(seg + j, _SEGMENTS, stride=pitch), :]
            return jnp.where(row >= 1, pltpu.roll(v, 1, 0), 0.0)
        v = xs[pl.ds(j - seg, _SEGMENTS, stride=pitch), :]
        return jnp.where(row < _SEGMENTS - 1, pltpu.roll(v, _SEGMENTS - 1, 0), 0.0)

    taps = [jnp.broadcast_to(cw_ref[i:i + 1, :], (_SEGMENTS, LRU_BLOCK)) for i in range(CONV_WIDTH)]
    bias = jnp.broadcast_to(cb_ref[...], (_SEGMENTS, LRU_BLOCK))
    per_chunk = _GATE_ROWS // _SEGMENTS

    p = q = None
    for r0 in range(0, seq, _GATE_ROWS):
        r1 = r0 + _GATE_ROWS
        j0 = r0 // _SEGMENTS
        vs = [positions(j) for j in range(j0 - 1, j0 + per_chunk + CONV_WIDTH - 2)]
        xc = jnp.concatenate(
            [((bias + vs[jj] * taps[0]) + vs[jj + 1] * taps[1] + vs[jj + 2] * taps[2])
             + vs[jj + 3] * taps[3] for jj in range(per_chunk)], axis=0)
        gh = jnp.dot(xc.astype(BF16), wg_ref[...], preferred_element_type=F32) + bg_ref[...]

        dp = jnp.dot(c_ref[:, r0:r1], ac_ref[r0:r1, :], preferred_element_type=F32)
        dq = jnp.dot(s_ref[:, r0:r1], as_ref[r0:r1, :], preferred_element_type=F32)
        p = dp if p is None else p + dp
        q = dq if q is None else q + dq

        (af, uf), (ab, ub) = _gate_terms(gh, xc, cq_ref[...])
        af_ref[r0:r1, :] = af
        uf_ref[r0:r1, :] = uf.astype(BF16)
        ab_ref[r0:r1, :] = ab
        ub_ref[r0:r1, :] = ub.astype(BF16)

    lo_ref[...] = (p[:tm] + q[:tm]).astype(BF16)
    mirrored = (p - q).astype(BF16)
    hi_ref[...] = jnp.dot(sel_ref[...], mirrored, preferred_element_type=F32).astype(BF16)


def _tdft_sym_tables(seq, tm):
    half_tiles = seq // 2 // tm
    ext = tm + _MIRROR_PAD
    k = (jnp.arange(half_tiles, dtype=jnp.int32)[:, None] * tm
         + jnp.arange(ext, dtype=jnp.int32)[None, :]).reshape(-1)
    side = math.isqrt(seq)
    assert side * side == seq
    tt = jnp.arange(side, dtype=jnp.int32)
    coarse = ((k[:, None] * tt[None, :]) % side).astype(F32) * (2.0 * math.pi / side)
    fine = ((k[:, None] * tt[None, :]) % seq).astype(F32) * (2.0 * math.pi / seq)
    t = jnp.arange(seq, dtype=jnp.int32)
    pick_coarse = (tt[:, None] == (t // side)[None, :]).astype(F32)
    pick_fine = (tt[:, None] == (t % side)[None, :]).astype(F32)
    expand = functools.partial(jnp.dot, precision=lax.Precision.HIGHEST)
    ca, sa = expand(jnp.cos(coarse), pick_coarse), expand(jnp.sin(coarse), pick_coarse)
    cb, sb = expand(jnp.cos(fine), pick_fine), expand(jnp.sin(fine), pick_fine)
    scale = 1.0 / math.sqrt(seq)
    cmat = ((ca * cb - sa * sb) * scale).astype(BF16).reshape(half_tiles, ext, seq)
    smat_neg = ((sa * cb + ca * sb) * (-scale)).astype(BF16).reshape(half_tiles, ext, seq)
    r = jnp.arange(tm, dtype=jnp.int32)
    sel = (jnp.arange(ext, dtype=jnp.int32)[None, :] == (tm - r)[:, None]).astype(BF16)
    return cmat, smat_neg, sel


def _tdft_gates(z16, a_s, rin, tables, conv_w, conv_b, wg, bg, cq, batch, seq, tm):
    cmat, smat_neg, sel = tables
    tn = COL_BLOCK // 2
    ac_block0 = SLOT_AC * COL_BLOCK // tn
    half_tiles = seq // 2 // tm
    col_tiles = FOURIER_WIDTH // tn
    assert col_tiles * half_tiles == LRU_HEADS
    ext = tm + _MIRROR_PAD
    n = batch * seq
    half = jax.ShapeDtypeStruct((n // 2, FOURIER_WIDTH), BF16)

    def head(c, i):
        return c * half_tiles + i

    head_cols = pl.BlockSpec((seq, LRU_BLOCK), lambda i, b, c: (b, head(c, i)))
    return pl.pallas_call(
        functools.partial(_tdft_gates_kernel, tm=tm, seq=seq),
        grid=(half_tiles, batch, col_tiles),
        in_specs=[
            pl.BlockSpec((None, ext, seq), lambda i, b, c: (i, 0, 0)),
            pl.BlockSpec((None, ext, seq), lambda i, b, c: (i, 0, 0)),
            pl.BlockSpec((seq, tn), lambda i, b, c: (b, ac_block0 + c)),
            pl.BlockSpec((seq, tn), lambda i, b, c: (b, c)),
            pl.BlockSpec((tm, ext), lambda i, b, c: (0, 0)),
            head_cols,
            pl.BlockSpec((CONV_WIDTH, LRU_BLOCK), lambda i, b, c: (0, head(c, i))),
            pl.BlockSpec((1, LRU_BLOCK), lambda i, b, c: (0, head(c, i))),
            pl.BlockSpec((None, LRU_BLOCK, 4 * LRU_BLOCK), lambda i, b, c: (head(c, i), 0, 0)),
            pl.BlockSpec((None, 1, 4 * LRU_BLOCK), lambda i, b, c: (head(c, i), 0, 0)),
            pl.BlockSpec((None, 1, 2 * LRU_BLOCK), lambda i, b, c: (head(c, i), 0, 0)),
        ],
        out_specs=[
            pl.BlockSpec((tm, tn), lambda i, b, c: (b * half_tiles + i, c)),
            pl.BlockSpec((tm, tn), lambda i, b, c: (b * half_tiles + half_tiles - 1 - i, c)),
            head_cols, head_cols, head_cols, head_cols,
        ],
        out_shape=[
            half, half,
            jax.ShapeDtypeStruct((n, LRU_WIDTH), F32),
            jax.ShapeDtypeStruct((n, LRU_WIDTH), BF16),
            jax.ShapeDtypeStruct((n, LRU_WIDTH), F32),
            jax.ShapeDtypeStruct((n, LRU_WIDTH), BF16),
        ],
        scratch_shapes=[pltpu.VMEM((_SEGMENTS * (seq // _SEGMENTS + _SEG_PAD), LRU_BLOCK), F32)],
        compiler_params=_params("parallel", "parallel", "parallel"),
        name="pos_dft_gates",
    )(cmat, smat_neg, z16, a_s, sel, rin, conv_w, conv_b, wg, bg, cq)


def _tdft_multi_kernel(c_ref, s_ref, ac_ref, as_ref, o_ref, *, nb, seq):
    for b in range(nb):
        r = slice(b * seq, (b + 1) * seq)
        o_ref[r, :] = (jnp.dot(c_ref[...], ac_ref[r, :], preferred_element_type=F32)
                       + jnp.dot(s_ref[...], as_ref[r, :], preferred_element_type=F32)).astype(BF16)


def _tdft_short_tables(seq):
    k = jnp.arange(seq, dtype=jnp.int32)
    ang = ((k[:, None] * k[None, :]) % seq).astype(F32) * (2.0 * math.pi / seq)
    scale = 1.0 / math.sqrt(seq)
    return (jnp.cos(ang) * scale).astype(BF16), (jnp.sin(ang) * (-scale)).astype(BF16)


def _tdft_short(z16, a_s, tables, batch, seq, nb):
    cmat, smat_neg = tables
    n = batch * seq
    tn = COL_BLOCK
    out_shape = jax.ShapeDtypeStruct((n, FOURIER_WIDTH), BF16)
    return pl.pallas_call(
        functools.partial(_tdft_multi_kernel, nb=nb, seq=seq),
        grid=(batch // nb, FOURIER_WIDTH // tn),
        in_specs=[
            pl.BlockSpec((seq, seq), lambda b, c: (0, 0)),
            pl.BlockSpec((seq, seq), lambda b, c: (0, 0)),
            pl.BlockSpec((nb * seq, tn), lambda b, c: (b, SLOT_AC + c)),
            pl.BlockSpec((nb * seq, tn), lambda b, c: (b, c)),
        ],
        out_specs=pl.BlockSpec((nb * seq, tn), lambda b, c: (b, c)),
        out_shape=out_shape,
        compiler_params=_params("parallel", "parallel"),
        name="pos_dft_small",
    )(cmat, smat_neg, z16, a_s)


def _fmix_kernel(flo_ref, fhi_ref, sf_ref, wm_ref, o_ref, *, half_tiles):
    f = flo_ref[...]
    if half_tiles is not None:
        in_low_half = (pl.program_id(0) % (2 * half_tiles)) < half_tiles
        f = jnp.where(in_low_half, f, fhi_ref[...])
    mix = jnp.dot(f, wm_ref[...], preferred_element_type=F32)
    o_ref[...] = (mix * sf_ref[...].astype(F32)).astype(BF16)


def _fmix(f_lo, f_hi, z16, w_fmix, layer, n, tm, half_tiles):
    fw = FOURIER_WIDTH
    if half_tiles is None:
        lo_map = lambda i: (i, 0)
        hi_map = lambda i: (0, 0)
    else:
        def lo_map(i):
            b, r = i // (2 * half_tiles), i % (2 * half_tiles)
            return (b * half_tiles + jnp.minimum(r, half_tiles - 1), 0)

        def hi_map(i):
            b, r = i // (2 * half_tiles), i % (2 * half_tiles)
            return (b * half_tiles + jnp.maximum(r - half_tiles, 0), 0)
    return pl.pallas_call(
        functools.partial(_fmix_kernel, half_tiles=half_tiles),
        grid=(n // tm,),
        in_specs=[
            pl.BlockSpec((tm, fw), lo_map),
            pl.BlockSpec((tm, fw), hi_map),
            pl.BlockSpec((tm, fw), lambda i: (i, SLOT_SF * COL_BLOCK // fw)),
            pl.BlockSpec((None, fw, fw), lambda i: (layer, 0, 0)),
        ],
        out_specs=pl.BlockSpec((tm, fw), lambda i: (i, 0)),
        out_shape=jax.ShapeDtypeStruct((n, fw), BF16),
        compiler_params=_params("parallel"),
        name="fourier_mix",
    )(f_lo, f_hi, z16, w_fmix)


def _merge_kernel(yf_ref, yr_ref, wf_ref, wr_ref, gf_ref, gr_ref, o_ref):
    pf = jnp.dot(yf_ref[...], wf_ref[...], preferred_element_type=F32)
    pr = jnp.dot(yr_ref[...], wr_ref[...], preferred_element_type=F32)
    o_ref[...] = (gf_ref[...].astype(F32) * pf + gr_ref[...].astype(F32) * pr).astype(BF16)


def _merge(yf, yr, z16, w_bf, w_br, layer, tm):
    n = yr.shape[0]
    tn = COL_BLOCK
    return pl.pallas_call(
        _merge_kernel,
        grid=(n // tm, D_MODEL // tn),
        in_specs=[
            pl.BlockSpec((tm, FOURIER_WIDTH), lambda i, j: (i, 0)),
            pl.BlockSpec((tm, LRU_WIDTH), lambda i, j: (i, 0)),
            pl.BlockSpec((None, FOURIER_WIDTH, tn), lambda i, j: (layer, 0, j)),
            pl.BlockSpec((None, LRU_WIDTH, tn), lambda i, j: (layer, 0, j)),
            pl.BlockSpec((tm, tn), lambda i, j: (i, SLOT_SGF + j)),
            pl.BlockSpec((tm, tn), lambda i, j: (i, SLOT_SGR + j)),
        ],
        out_specs=pl.BlockSpec((tm, tn), lambda i, j: (i, j)),
        out_shape=jax.ShapeDtypeStruct((n, D_MODEL), BF16),
        compiler_params=_params("parallel", "parallel"),
        name="branch_merge",
    )(yf, yr, w_bf, w_br, z16, z16)


_OUT_COLS = D_MODEL // COL_BLOCK


def _row_norm(xrow, ng_ref):
    ssq = None
    for c in range(_OUT_COLS):
        v = xrow[c]
        part = jnp.sum(v * v, axis=-1, keepdims=True)
        ssq = part if ssq is None else ssq + part
    inv = lax.rsqrt(ssq * (1.0 / D_MODEL) + EPS)
    for c in range(_OUT_COLS):
        cols = slice(c * COL_BLOCK, (c + 1) * COL_BLOCK)
        yield cols, (xrow[c] * inv) * ng_ref[:, cols]


def _outproj_mid_kernel(m_ref, w_ref, x_ref, g_ref, ng_ref, sc_ref, sh_ref, xo_ref, h_ref, xrow):
    j = pl.program_id(1)
    xn = x_ref[...] + g_ref[...] * jnp.dot(m_ref[...], w_ref[...], preferred_element_type=F32)
    xo_ref[...] = xn
    xrow[j] = xn

    @pl.when(j == _OUT_COLS - 1)
    def _():
        for cols, y in _row_norm(xrow, ng_ref):
            h_ref[:, cols] = (y * (1.0 + sc_ref[:, cols]) + sh_ref[:, cols]).astype(BF16)


def _outproj_last_kernel(m_ref, w_ref, x_ref, g_ref, ng_ref, y_ref, xrow):
    j = pl.program_id(1)
    xrow[j] = x_ref[...] + g_ref[...] * jnp.dot(m_ref[...], w_ref[...], preferred_element_type=F32)

    @pl.when(j == _OUT_COLS - 1)
    def _():
        for cols, y in _row_norm(xrow, ng_ref):
            y_ref[:, cols] = y


def _outproj(merged, w_out, layer, x, gate, row_of, tm, norm_g, scale=None, shift=None):
    n = x.shape[0]
    tn = COL_BLOCK
    last = scale is None
    row_block = pl.BlockSpec((tm, D_MODEL), lambda i, j: (i, 0))
    col_block = pl.BlockSpec((tm, tn), lambda i, j: (i, j))
    mod_row = pl.BlockSpec((None, 1, D_MODEL), lambda i, j: (row_of(i), 0, 0))
    in_specs = [
        row_block,
        pl.BlockSpec((None, D_MODEL, tn), lambda i, j: (layer, 0, j)),
        col_block,
        pl.BlockSpec((None, 1, tn), lambda i, j: (row_of(i), 0, j)),
        pl.BlockSpec((1, D_MODEL), lambda i, j: (0, 0)),
    ]
    scratch = [pltpu.VMEM((_OUT_COLS, tm, tn), F32)]
    if last:
        return pl.pallas_call(
            _outproj_last_kernel,
            grid=(n // tm, _OUT_COLS),
            in_specs=in_specs,
            out_specs=row_block,
            out_shape=jax.ShapeDtypeStruct((n, D_MODEL), F32),
            scratch_shapes=scratch,
            compiler_params=_params("parallel", "arbitrary"),
            name="out_proj_final_norm",
        )(merged, w_out, x, gate, norm_g)
    return pl.pallas_call(
        _outproj_mid_kernel,
        grid=(n // tm, _OUT_COLS),
        in_specs=in_specs + [mod_row, mod_row],
        out_specs=[col_block, row_block],
        out_shape=[jax.ShapeDtypeStruct((n, D_MODEL), F32),
                   jax.ShapeDtypeStruct((n, D_MODEL), BF16)],
        scratch_shapes=scratch,
        compiler_params=_params("parallel", "arbitrary"),
        name="out_proj_norm",
    )(merged, w_out, x, gate, norm_g, scale, shift)


class _Tiles(NamedTuple):
    norm: int
    inproj: int
    merge: int
    outproj: int
    lru_channels: int
    dft_rows: Optional[int]
    dft_batch: Optional[int]


_LONG_SEQ = 1024


def _tiles(seq):
    if seq >= _LONG_SEQ:
        return _Tiles(norm=512, inproj=1024, merge=1024, outproj=512, lru_channels=256,
                      dft_rows=512, dft_batch=None)
    return _Tiles(norm=512, inproj=1024, merge=1024, outproj=512, lru_channels=LRU_WIDTH,
                  dft_rows=None, dft_batch=8)


def _channel_dft_table():
    k = jnp.arange(FOURIER_GROUP_CH, dtype=jnp.int32)
    ang = ((k[:, None] * k[None, :]) % FOURIER_GROUP_CH).astype(F32) * (2.0 * math.pi / FOURIER_GROUP_CH)
    scale = 1.0 / math.sqrt(FOURIER_GROUP_CH)
    return jnp.concatenate([jnp.cos(ang) * scale, jnp.sin(ang) * scale], axis=1).astype(BF16)


def _per_head(v, parts):
    return (v.reshape(parts, LRU_HEADS, LRU_BLOCK).transpose(1, 0, 2)
            .reshape(LRU_HEADS, 1, parts * LRU_BLOCK))


def _stream(x, batch, seq, row_of_token_tile, h0_layers, mods, big, small, cs, final_g):
    n = batch * seq
    x = x.reshape(n, D_MODEL)
    tiles = _tiles(seq)
    long_seq = tiles.dft_rows is not None
    if long_seq:
        dft_tables = _tdft_sym_tables(seq, tiles.dft_rows)
        half_tiles = seq // 2 // tiles.merge
    else:
        dft_tables = _tdft_short_tables(seq)
        half_tiles = None
    states = []
    shift, scale, _ = mods[0]
    h = _norm_mod(x, small[0]["norm_g"], scale, shift, row_of_token_tile(tiles.norm), tiles.norm)
    for l in range(DEPTH):
        w = small[l]
        gate = mods[l][2]
        z16, a_s, rin = _inproj(h, big["w_in"], l, cs, tiles.inproj)
        if long_seq:
            f_lo, f_hi, a_f, u_f, a_b, u_b = _tdft_gates(
                z16, a_s, rin, dft_tables, w["conv_w"], w["conv_b"], w["wg"], w["bg"], w["cq"],
                batch, seq, tiles.dft_rows)
            yr, st = _scan(a_f, u_f, a_b, u_b, z16, h0_layers[l], batch, seq, tiles.lru_channels)
        else:
            f_lo = f_hi = _tdft_short(z16, a_s, dft_tables, batch, seq, tiles.dft_batch)
            yr, st = _lru(rin, z16, w["conv_w"], w["conv_b"], w["wg"], w["bg"], w["cq"],
                          h0_layers[l], batch, seq, tiles.lru_channels)
        states.append(st)
        yf = _fmix(f_lo, f_hi, z16, big["w_fmix"], l, n, tiles.merge, half_tiles)
        merged = _merge(yf, yr, z16, big["w_bf"], big["w_br"], l, tiles.merge)
        row_of = row_of_token_tile(tiles.outproj)
        if l + 1 < DEPTH:
            next_shift, next_scale, _ = mods[l + 1]
            x, h = _outproj(merged, big["w_out"], l, x, gate, row_of, tiles.outproj,
                            small[l + 1]["norm_g"], next_scale, next_shift)
        else:
            y = _outproj(merged, big["w_out"], l, x, gate, row_of, tiles.outproj, final_g)
    return y.reshape(batch, seq, D_MODEL), states


def kernel(x_prompt, x_sample, state_lru, c, c_ctx, norm_g, w_ada, b_ada, w_in, w_fmix,
           conv_w, conv_b, w_gate_a, b_gate_a, w_gate_x, b_gate_x, lam,
           w_branch_f, w_branch_r, w_out, final_g):
    batch, seq, _ = x_prompt.shape
    dec_batch, dec_seq, _ = x_sample.shape
    assert 1 + dec_batch <= COND_ROWS

    cond = jnp.zeros((COND_ROWS, D_MODEL), F32)
    cond = cond.at[0].set(c_ctx).at[1:1 + dec_batch].set(c)
    mod = _ada(cond, w_ada, b_ada)
    mods = []
    for l in range(DEPTH):
        parts = jnp.split(mod[l], 3, axis=-1)
        mods.append(tuple(p.reshape(COND_ROWS, 1, D_MODEL) for p in parts))

    big = dict(w_in=w_in.astype(BF16), w_fmix=w_fmix.astype(BF16), w_bf=w_branch_f.astype(BF16),
               w_br=w_branch_r.astype(BF16), w_out=w_out.astype(BF16))
    small = []
    for l in range(DEPTH):
        wg = (0.5 * jnp.concatenate([w_gate_a[l, 0], w_gate_x[l, 0], w_gate_a[l, 1], w_gate_x[l, 1]],
                                    axis=-1)).astype(BF16)
        bg = _per_head(0.5 * jnp.stack([b_gate_a[l, 0], b_gate_x[l, 0], b_gate_a[l, 1], b_gate_x[l, 1]]), 4)
        cq = _per_head((0.25 * LRU_C) * jax.nn.softplus(-lam[l]), 2)
        small.append(dict(norm_g=norm_g[l].reshape(1, D_MODEL), conv_w=conv_w[l],
                          conv_b=conv_b[l].reshape(1, LRU_WIDTH), wg=wg, bg=bg, cq=cq))

    cs = _channel_dft_table()
    final_g = final_g.reshape(1, D_MODEL)

    zeros_h0 = jnp.zeros((batch, 2, LRU_WIDTH), F32)
    y_prompt, ctx_states = _stream(
        x_prompt, batch, seq, lambda tm: (lambda i: 0), [zeros_h0] * DEPTH,
        mods, big, small, cs, final_g)
    new_state = jnp.stack(ctx_states, axis=1)

    y_sample, _ = _stream(
        x_sample, dec_batch, dec_seq, lambda tm: (lambda i: 1 + (i * tm) // dec_seq),
        [state_lru[:, l] for l in range(DEPTH)], mods, big, small, cs, final_g)
    return (y_prompt, y_sample, new_state)
```

```python
import functools
import math
from typing import NamedTuple, Optional

import jax
import jax.numpy as jnp
from jax import lax
from jax.experimental import pallas as pl
from jax.experimental.pallas import tpu as pltpu

F32 = jnp.float32
BF16 = jnp.bfloat16

D_MODEL = 4096
DEPTH = 2
FOURIER_WIDTH = D_MODEL // 2
FOURIER_GROUP_CH = 256
LRU_WIDTH = D_MODEL // 2
LRU_BLOCK = 128
LRU_HEADS = LRU_WIDTH // LRU_BLOCK
CONV_WIDTH = 4
LRU_C = 8.0
IN_WIDTH = 2 * FOURIER_WIDTH + 2 * LRU_WIDTH + 2 * D_MODEL
EPS = 1e-6

SUBLANES = 8
LANES = 128
VMEM_LIMIT_BYTES = 58 * 1024 * 1024
F32_MIN_NORMAL = 2.0 ** -126

COND_ROWS = 16
COL_BLOCK = 1024
SLOT_AC, SLOT_SF, SLOT_SR, SLOT_SGF, SLOT_SGR = 0, 2, 4, 6, 10
PACKED_BLOCKS = 14


def _sigmoid(x):
    return 0.5 * jnp.tanh(0.5 * x) + 0.5


def _params(*sem):
    return pltpu.CompilerParams(dimension_semantics=sem, vmem_limit_bytes=VMEM_LIMIT_BYTES)


def _ada_kernel(c_ref, w_ref, b_ref, o_ref):
    c = c_ref[...]
    s = (c * _sigmoid(c)).astype(BF16)
    w = w_ref[...].astype(BF16)
    o_ref[...] = jnp.dot(s, w, preferred_element_type=F32) + b_ref[...]


def _ada(cond, w_ada, b_ada):
    tn = 512
    n_out = 3 * D_MODEL
    return pl.pallas_call(
        _ada_kernel,
        grid=(DEPTH, n_out // tn),
        in_specs=[
            pl.BlockSpec((COND_ROWS, D_MODEL), lambda l, j: (0, 0)),
            pl.BlockSpec((None, D_MODEL, tn), lambda l, j: (l, 0, j)),
            pl.BlockSpec((None, 1, tn), lambda l, j: (l, 0, j)),
        ],
        out_specs=pl.BlockSpec((None, COND_ROWS, tn), lambda l, j: (l, 0, j)),
        out_shape=jax.ShapeDtypeStruct((DEPTH, COND_ROWS, n_out), F32),
        compiler_params=_params("parallel", "parallel"),
        name="ada_mod",
    )(cond, w_ada, b_ada.reshape(DEPTH, 1, n_out))


def _norm_mod_kernel(x_ref, g_ref, sc_ref, sh_ref, o_ref):
    x = x_ref[...]
    ms = jnp.mean(x * x, axis=-1, keepdims=True)
    y = (x * lax.rsqrt(ms + EPS)) * g_ref[...]
    o_ref[...] = (y * (1.0 + sc_ref[...]) + sh_ref[...]).astype(o_ref.dtype)


def _norm_mod(x, g, scale, shift, row_of, tm):
    n = x.shape[0]
    return pl.pallas_call(
        _norm_mod_kernel,
        grid=(n // tm,),
        in_specs=[
            pl.BlockSpec((tm, D_MODEL), lambda i: (i, 0)),
            pl.BlockSpec((1, D_MODEL), lambda i: (0, 0)),
            pl.BlockSpec((None, 1, D_MODEL), lambda i: (row_of(i), 0, 0)),
            pl.BlockSpec((None, 1, D_MODEL), lambda i: (row_of(i), 0, 0)),
        ],
        out_specs=pl.BlockSpec((tm, D_MODEL), lambda i: (i, 0)),
        out_shape=jax.ShapeDtypeStruct((n, D_MODEL), BF16),
        compiler_params=_params("parallel"),
        name="norm_mod",
    )(x, g, scale, shift)


def _inproj_kernel(h_ref, w_ref, cs_ref, z16_ref, as_ref, rin_ref):
    j = pl.program_id(1)

    def z():
        return jnp.dot(h_ref[...], w_ref[...], preferred_element_type=F32)

    @pl.when(j < 2)
    def _():
        zb = z().astype(BF16)
        for g in range(COL_BLOCK // FOURIER_GROUP_CH):
            cols = slice(g * FOURIER_GROUP_CH, (g + 1) * FOURIER_GROUP_CH)
            t = jnp.dot(zb[:, cols], cs_ref[...], preferred_element_type=F32)
            z16_ref[:, cols] = t[:, :FOURIER_GROUP_CH].astype(BF16)
            as_ref[:, cols] = t[:, FOURIER_GROUP_CH:].astype(BF16)

    @pl.when(((j >= 2) & (j < 4)) | ((j >= 6) & (j < 8)))
    def _():
        v = z()
        z16_ref[...] = (v * _sigmoid(v)).astype(BF16)

    @pl.when((j >= 4) & (j < 6))
    def _():
        rin_ref[...] = z()

    @pl.when(j >= 8)
    def _():
        z16_ref[...] = _sigmoid(z()).astype(BF16)


def _packed_slot(j):
    return jnp.where(j < 4, j, jnp.where(j < 6, 3, j - 2))


def _inproj(h, w_in, layer, cs, tm):
    n = h.shape[0]
    tn = COL_BLOCK
    return pl.pallas_call(
        _inproj_kernel,
        grid=(n // tm, IN_WIDTH // tn),
        in_specs=[
            pl.BlockSpec((tm, D_MODEL), lambda i, j: (i, 0)),
            pl.BlockSpec((None, D_MODEL, tn), lambda i, j: (layer, 0, j)),
            pl.BlockSpec((FOURIER_GROUP_CH, 2 * FOURIER_GROUP_CH), lambda i, j: (0, 0)),
        ],
        out_specs=[
            pl.BlockSpec((tm, tn), lambda i, j: (i, _packed_slot(j))),
            pl.BlockSpec((tm, tn), lambda i, j: (i, jnp.minimum(j, 1))),
            pl.BlockSpec((tm, tn), lambda i, j: (i, jnp.clip(j - 4, 0, 1))),
        ],
        out_shape=[
            jax.ShapeDtypeStruct((n, PACKED_BLOCKS * COL_BLOCK), BF16),
            jax.ShapeDtypeStruct((n, FOURIER_WIDTH), BF16),
            jax.ShapeDtypeStruct((n, LRU_WIDTH), F32),
        ],
        compiler_params=_params("parallel", "arbitrary"),
        name="in_proj",
    )(h, w_in, cs)


_HALO = SUBLANES
_GATE_ROWS = 256
_SEGMENTS = SUBLANES
_SEG_PAD = SUBLANES


def _conv_taps(xe, cw_ref, cb_ref, rows):
    ext = rows + 2 * _HALO
    inner = slice(_HALO, _HALO + rows)
    xc = cb_ref[...] + pltpu.roll(xe, 1, 0)[inner] * cw_ref[0:1, :]
    xc = xc + xe[inner] * cw_ref[1:2, :]
    xc = xc + pltpu.roll(xe, ext - 1, 0)[inner] * cw_ref[2:3, :]
    return xc + pltpu.roll(xe, ext - 2, 0)[inner] * cw_ref[3:4, :]


def _gate_terms(gh, xh, cq):
    out = []
    for d in range(2):
        base = 2 * d * LRU_BLOCK
        t_a = jnp.tanh(gh[:, base:base + LRU_BLOCK])
        t_x = jnp.tanh(gh[:, base + LRU_BLOCK:base + 2 * LRU_BLOCK])
        c = cq[:, d * LRU_BLOCK:(d + 1) * LRU_BLOCK]
        s = jnp.tanh(c * t_a + c)
        inv = 1.0 / (1.0 + s)
        root = s * lax.rsqrt(jnp.maximum(s, F32_MIN_NORMAL))
        a = (1.0 - s) * inv
        u = (inv * root) * (xh * t_x + xh)
        out.append((a, u))
    return out


def _block_scan(a, b, row, reverse):
    for s in (1, 2, 4):
        if reverse:
            m, sh = row < SUBLANES - s, SUBLANES - s
        else:
            m, sh = row >= s, s
        a_sh = jnp.where(m, pltpu.roll(a, sh, 0), 1.0)
        b_sh = jnp.where(m, pltpu.roll(b, sh, 0), 0.0)
        b = b + a * b_sh
        a = a * a_sh
    return a, b


_SCAN_ROWS = 2 * SUBLANES


def _scan_both(a_f, u_f, a_b, u_b, h_f, h_b, h0_ref, st_ref, *, seq, tc, cs):
    row = lax.broadcasted_iota(jnp.int32, (SUBLANES, cs), 0)
    for c0 in range(0, tc, cs):
        cols = slice(c0, c0 + cs)

        def body(k, carry, cols=cols):
            carry_f, carry_b = carry
            rf = pl.multiple_of(k * _SCAN_ROWS, _SCAN_ROWS)
            rb = pl.multiple_of(seq - _SCAN_ROWS - k * _SCAN_ROWS, _SCAN_ROWS)
            a2 = a_f[pl.ds(rf, _SCAN_ROWS), cols]
            b2 = u_f[pl.ds(rf, _SCAN_ROWS), cols].astype(F32)
            hs = []
            for p in (0, 1):
                blk = slice(p * SUBLANES, (p + 1) * SUBLANES)
                a, b = _block_scan(a2[blk], b2[blk], row, False)
                h = a * carry_f + b
                hs.append(h)
                carry_f = jnp.broadcast_to(h[SUBLANES - 1:SUBLANES, :], (SUBLANES, cs))
            h_f[pl.ds(rf, _SCAN_ROWS), cols] = jnp.concatenate(hs, axis=0)

            a2 = a_b[pl.ds(rb, _SCAN_ROWS), cols]
            b2 = u_b[pl.ds(rb, _SCAN_ROWS), cols].astype(F32)
            hs = [None, None]
            for p in (1, 0):
                blk = slice(p * SUBLANES, (p + 1) * SUBLANES)
                a, b = _block_scan(a2[blk], b2[blk], row, True)
                h = a * carry_b + b
                hs[p] = h
                carry_b = jnp.broadcast_to(h[0:1, :], (SUBLANES, cs))
            h_b[pl.ds(rb, _SCAN_ROWS), cols] = jnp.concatenate(hs, axis=0)
            return carry_f, carry_b

        init = (jnp.broadcast_to(h0_ref[0:1, cols], (SUBLANES, cs)),
                jnp.broadcast_to(h0_ref[1:2, cols], (SUBLANES, cs)))
        fin_f, fin_b = lax.fori_loop(0, seq // _SCAN_ROWS, body, init)
        st_ref[0:1, cols] = fin_f[0:1, :]
        st_ref[1:2, cols] = fin_b[0:1, :]


def _write_gated_sum(h_f, h_b, sr_ref, y_ref, seq):
    def body(c, carry):
        r = pl.ds(pl.multiple_of(c * _GATE_ROWS, _GATE_ROWS), _GATE_ROWS)
        y_ref[r, :] = ((h_f[r, :] + h_b[r, :]) * sr_ref[r, :].astype(F32)).astype(BF16)
        return carry
    lax.fori_loop(0, seq // _GATE_ROWS, body, 0)


def _lru_kernel(rin_ref, sr_ref, cw_ref, cb_ref, wg_ref, bg_ref, cq_ref, h0_ref,
                y_ref, st_ref, xpad, a_f, u_f, a_b, u_b, *, seq, tc, cs):
    heads = tc // LRU_BLOCK
    n_chunks = seq // _GATE_ROWS

    def rows(c):
        return pl.multiple_of(c * _GATE_ROWS, _GATE_ROWS)

    zero_halo = jnp.zeros((_HALO, tc), F32)
    xpad[0:_HALO, :] = zero_halo
    xpad[seq + _HALO:seq + 2 * _HALO, :] = zero_halo

    def copy_body(c, carry):
        r0 = rows(c)
        xpad[pl.ds(r0 + _HALO, _GATE_ROWS), :] = rin_ref[pl.ds(r0, _GATE_ROWS), :]
        return carry
    lax.fori_loop(0, n_chunks, copy_body, 0)

    def gate_body(c, carry):
        r0 = rows(c)
        xc = _conv_taps(xpad[pl.ds(r0, _GATE_ROWS + 2 * _HALO), :], cw_ref, cb_ref, _GATE_ROWS)
        for hh in range(heads):
            cols = slice(hh * LRU_BLOCK, (hh + 1) * LRU_BLOCK)
            xh = xc[:, cols]
            gh = jnp.dot(xh.astype(BF16), wg_ref[hh], preferred_element_type=F32) + bg_ref[hh]
            (af, uf), (ab, ub) = _gate_terms(gh, xh, cq_ref[hh])
            a_f[pl.ds(r0, _GATE_ROWS), cols] = af
            u_f[pl.ds(r0, _GATE_ROWS), cols] = uf
            a_b[pl.ds(r0, _GATE_ROWS), cols] = ab
            u_b[pl.ds(r0, _GATE_ROWS), cols] = ub
        return carry
    lax.fori_loop(0, n_chunks, gate_body, 0)

    _scan_both(a_f, u_f, a_b, u_b, a_f, a_b, h0_ref, st_ref, seq=seq, tc=tc, cs=cs)
    _write_gated_sum(a_f, a_b, sr_ref, y_ref, seq)


def _lru(rin, z16, conv_w, conv_b, wg, bg, cq, h0, batch, seq, tc):
    n = batch * seq
    heads = tc // LRU_BLOCK
    cs = min(tc, 512)
    sr_block0 = SLOT_SR * COL_BLOCK // tc
    kern = functools.partial(_lru_kernel, seq=seq, tc=tc, cs=cs)
    return pl.pallas_call(
        kern,
        grid=(batch, LRU_WIDTH // tc),
        in_specs=[
            pl.BlockSpec((seq, tc), lambda b, c: (b, c)),
            pl.BlockSpec((seq, tc), lambda b, c: (b, sr_block0 + c)),
            pl.BlockSpec((CONV_WIDTH, tc), lambda b, c: (0, c)),
            pl.BlockSpec((1, tc), lambda b, c: (0, c)),
            pl.BlockSpec((heads, LRU_BLOCK, 4 * LRU_BLOCK), lambda b, c: (c, 0, 0)),
            pl.BlockSpec((heads, 1, 4 * LRU_BLOCK), lambda b, c: (c, 0, 0)),
            pl.BlockSpec((heads, 1, 2 * LRU_BLOCK), lambda b, c: (c, 0, 0)),
            pl.BlockSpec((None, 2, tc), lambda b, c: (b, 0, c)),
        ],
        out_specs=[
            pl.BlockSpec((seq, tc), lambda b, c: (b, c)),
            pl.BlockSpec((None, 2, tc), lambda b, c: (b, 0, c)),
        ],
        out_shape=[
            jax.ShapeDtypeStruct((n, LRU_WIDTH), BF16),
            jax.ShapeDtypeStruct((batch, 2, LRU_WIDTH), F32),
        ],
        scratch_shapes=[
            pltpu.VMEM((seq + 2 * _HALO, tc), F32),
            pltpu.VMEM((seq, tc), F32),
            pltpu.VMEM((seq, tc), F32),
            pltpu.VMEM((seq, tc), F32),
            pltpu.VMEM((seq, tc), F32),
        ],
        compiler_params=_params("parallel", "parallel"),
        name="conv_rglru",
    )(rin, z16, conv_w, conv_b, wg, bg, cq, h0)


def _segment_permutation():
    per_chunk = _GATE_ROWS // _SEGMENTS
    r = jnp.arange(_GATE_ROWS, dtype=jnp.int32)
    src = (r % _SEGMENTS) * per_chunk + r // _SEGMENTS
    return (jnp.arange(_GATE_ROWS, dtype=jnp.int32)[None, :] == src[:, None]).astype(BF16)


_MIRROR_PAD = 16


def _tdft_lru_kernel(c_ref, s_ref, ac_ref, as_ref, sel_ref, rin_ref, cw_ref, cb_ref, wg_ref,
                     bg_ref, cq_ref, sr_ref, h0_ref, pm_ref, pmt_ref, lo_ref, hi_ref, y_ref,
                     xs, a_f, u_f, a_b, u_b, *, tm, seq):
    seg = seq // _SEGMENTS
    pitch = seg + _SEG_PAD
    for s in range(_SEGMENTS):
        xs[s * pitch:s * pitch + seg, :] = rin_ref[s * seg:(s + 1) * seg, :]
    row = lax.broadcasted_iota(jnp.int32, (_SEGMENTS, LRU_BLOCK), 0)

    def positions(j):
        if 0 <= j < seg:
            return xs[pl.ds(j, _SEGMENTS, stride=pitch), :]
        if j < 0:
            v = xs[pl.ds(seg + j, _SEGMENTS, stride=pitch), :]
            return jnp.where(row >= 1, pltpu.roll(v, 1, 0), 0.0)
        v = xs[pl.ds(j - seg, _SEGMENTS, stride=pitch), :]
        return jnp.where(row < _SEGMENTS - 1, pltpu.roll(v, _SEGMENTS - 1, 0), 0.0)

    taps = [jnp.broadcast_to(cw_ref[i:i + 1, :], (_SEGMENTS, LRU_BLOCK)) for i in range(CONV_WIDTH)]
    bias = jnp.broadcast_to(cb_ref[...], (_SEGMENTS, LRU_BLOCK))
    per_chunk = _GATE_ROWS // _SEGMENTS

    n_chunks = seq // _GATE_ROWS
    blocks = [slice(jj * _SEGMENTS, (jj + 1) * _SEGMENTS) for jj in range(per_chunk)]

    def chunk_rows(k):
        return slice(k * _GATE_ROWS, (k + 1) * _GATE_ROWS)

    def chunk_slabs(k):
        return [slice(s * seg + k * per_chunk, s * seg + (k + 1) * per_chunk)
                for s in range(_SEGMENTS)]

    def dft_slice(k, acc):
        r = chunk_rows(k)
        dp = jnp.dot(c_ref[:, r], ac_ref[r, :], preferred_element_type=F32)
        dq = jnp.dot(s_ref[:, r], as_ref[r, :], preferred_element_type=F32)
        return (dp, dq) if acc is None else (acc[0] + dp, acc[1] + dq)

    late_slices = n_chunks // 4
    early_slices = n_chunks - late_slices

    zeros = jnp.zeros((_SEGMENTS, LRU_BLOCK), F32)
    ones = jnp.ones((_SEGMENTS, LRU_BLOCK), F32)
    end_f, prod_f, end_b, prod_b = zeros, ones, zeros, ones
    acc = None
    for k in range(n_chunks):
        r = chunk_rows(k)
        j0 = k * per_chunk
        vs = [positions(j) for j in range(j0 - 1, j0 + per_chunk + CONV_WIDTH - 2)]
        xc = jnp.concatenate(
            [((bias + vs[jj] * taps[0]) + vs[jj + 1] * taps[1] + vs[jj + 2] * taps[2])
             + vs[jj + 3] * taps[3] for jj in range(per_chunk)], axis=0)
        gh = jnp.dot(xc.astype(BF16), wg_ref[...], preferred_element_type=F32) + bg_ref[...]
        if k < early_slices:
            acc = dft_slice(k, acc)

        (af, uf), (ab, ub) = _gate_terms(gh, xc, cq_ref[...])
        a_f[r, :] = af
        u_f[r, :] = uf
        a_b[r, :] = ab
        u_b[r, :] = ub
        for blk in blocks:
            end_f = af[blk] * end_f + uf[blk]
            prod_f = prod_f * af[blk]
            end_b = end_b + prod_b * ub[blk]
            prod_b = prod_b * ab[blk]

    h0_f = jnp.broadcast_to(h0_ref[0:1, :], (_SEGMENTS, LRU_BLOCK))
    h0_b = jnp.broadcast_to(h0_ref[1:2, :], (_SEGMENTS, LRU_BLOCK))
    a_cum, b_cum = _block_scan(prod_f, end_f, row, False)
    true_end_f = a_cum * h0_f + b_cum
    h_fwd = jnp.where(row >= 1, pltpu.roll(true_end_f, 1, 0), h0_f)
    a_cum, b_cum = _block_scan(prod_b, end_b, row, True)
    true_end_b = a_cum * h0_b + b_cum
    h_bwd = jnp.where(row < _SEGMENTS - 1, pltpu.roll(true_end_b, _SEGMENTS - 1, 0), h0_b)

    def emit_y(k):
        r = chunk_rows(k)
        sr_pos = jnp.concatenate([sr_ref[sl, :] for sl in chunk_slabs(k)], axis=0)
        sr_seg = jnp.dot(pm_ref[...], sr_pos, preferred_element_type=F32)
        y_seg = ((u_f[r, :] + u_b[r, :]) * sr_seg).astype(BF16)
        y_pos = jnp.dot(pmt_ref[...], y_seg, preferred_element_type=F32).astype(BF16)
        for s, sl in enumerate(chunk_slabs(k)):
            y_ref[sl, :] = y_pos[s * per_chunk:(s + 1) * per_chunk, :]

    half_chunks = n_chunks // 2
    late_every = half_chunks // late_slices
    for k in range(n_chunks):
        if k < half_chunks and k % late_every == 0:
            acc = dft_slice(early_slices + k // late_every, acc)
        if k == half_chunks:
            p, q = acc
            lo_ref[...] = (p[:tm] + q[:tm]).astype(BF16)
            mirrored = (p - q).astype(BF16)
            hi_ref[...] = jnp.dot(sel_ref[...], mirrored, preferred_element_type=F32).astype(BF16)
        rf, rb = chunk_rows(k), chunk_rows(n_chunks - 1 - k)
        af_c, uf_c, ab_c, ub_c = a_f[rf, :], u_f[rf, :], a_b[rb, :], u_b[rb, :]
        hfs, hbs = [], [None] * per_chunk
        for jj in range(per_chunk):
            h_fwd = af_c[blocks[jj]] * h_fwd + uf_c[blocks[jj]]
            hfs.append(h_fwd)
            jb = per_chunk - 1 - jj
            h_bwd = ab_c[blocks[jb]] * h_bwd + ub_c[blocks[jb]]
            hbs[jb] = h_bwd
        u_f[rf, :] = jnp.concatenate(hfs, axis=0)
        u_b[rb, :] = jnp.concatenate(hbs, axis=0)
        if k >= half_chunks:
            emit_y(k)
            emit_y(n_chunks - 1 - k)


def _tdft_sym_tables(seq, tm):
    half_tiles = seq // 2 // tm
    ext = tm + _MIRROR_PAD
    k = (jnp.arange(half_tiles, dtype=jnp.int32)[:, None] * tm
         + jnp.arange(ext, dtype=jnp.int32)[None, :]).reshape(-1)
    side = math.isqrt(seq)
    assert side * side == seq
    tt = jnp.arange(side, dtype=jnp.int32)
    coarse = ((k[:, None] * tt[None, :]) % side).astype(F32) * (2.0 * math.pi / side)
    fine = ((k[:, None] * tt[None, :]) % seq).astype(F32) * (2.0 * math.pi / seq)
    t = jnp.arange(seq, dtype=jnp.int32)
    pick_coarse = (tt[:, None] == (t // side)[None, :]).astype(F32)
    pick_fine = (tt[:, None] == (t % side)[None, :]).astype(F32)
    expand = functools.partial(jnp.dot, precision=lax.Precision.HIGHEST)
    ca, sa = expand(jnp.cos(coarse), pick_coarse), expand(jnp.sin(coarse), pick_coarse)
    cb, sb = expand(jnp.cos(fine), pick_fine), expand(jnp.sin(fine), pick_fine)
    scale = 1.0 / math.sqrt(seq)
    cmat = ((ca * cb - sa * sb) * scale).astype(BF16).reshape(half_tiles, ext, seq)
    smat_neg = ((sa * cb + ca * sb) * (-scale)).astype(BF16).reshape(half_tiles, ext, seq)
    r = jnp.arange(tm, dtype=jnp.int32)
    sel = (jnp.arange(ext, dtype=jnp.int32)[None, :] == (tm - r)[:, None]).astype(BF16)
    return cmat, smat_neg, sel


def _tdft_lru(z16, a_s, rin, tables, conv_w, conv_b, wg, bg, cq, h0, batch, seq, tm):
    cmat, smat_neg, sel = tables
    tn = COL_BLOCK // 2
    ac_block0 = SLOT_AC * COL_BLOCK // tn
    sr_block0 = SLOT_SR * COL_BLOCK // LRU_BLOCK
    perm = _segment_permutation()
    half_tiles = seq // 2 // tm
    col_tiles = FOURIER_WIDTH // tn
    assert col_tiles * half_tiles == LRU_HEADS
    ext = tm + _MIRROR_PAD
    n = batch * seq
    half = jax.ShapeDtypeStruct((n // 2, FOURIER_WIDTH), BF16)

    def head(c, i):
        return c * half_tiles + i

    head_cols = pl.BlockSpec((seq, LRU_BLOCK), lambda i, b, c: (b, head(c, i)))
    perm_spec = pl.BlockSpec((_GATE_ROWS, _GATE_ROWS), lambda i, b, c: (0, 0))
    head_f32 = pltpu.VMEM((seq, LRU_BLOCK), F32)
    return pl.pallas_call(
        functools.partial(_tdft_lru_kernel, tm=tm, seq=seq),
        grid=(half_tiles, batch, col_tiles),
        in_specs=[
            pl.BlockSpec((None, ext, seq), lambda i, b, c: (i, 0, 0)),
            pl.BlockSpec((None, ext, seq), lambda i, b, c: (i, 0, 0)),
            pl.BlockSpec((seq, tn), lambda i, b, c: (b, ac_block0 + c)),
            pl.BlockSpec((seq, tn), lambda i, b, c: (b, c)),
            pl.BlockSpec((tm, ext), lambda i, b, c: (0, 0)),
            head_cols,
            pl.BlockSpec((CONV_WIDTH, LRU_BLOCK), lambda i, b, c: (0, head(c, i))),
            pl.BlockSpec((1, LRU_BLOCK), lambda i, b, c: (0, head(c, i))),
            pl.BlockSpec((None, LRU_BLOCK, 4 * LRU_BLOCK), lambda i, b, c: (head(c, i), 0, 0)),
            pl.BlockSpec((None, 1, 4 * LRU_BLOCK), lambda i, b, c: (head(c, i), 0, 0)),
            pl.BlockSpec((None, 1, 2 * LRU_BLOCK), lambda i, b, c: (head(c, i), 0, 0)),
            pl.BlockSpec((seq, LRU_BLOCK), lambda i, b, c: (b, sr_block0 + head(c, i))),
            pl.BlockSpec((None, 2, LRU_BLOCK), lambda i, b, c: (b, 0, head(c, i))),
            perm_spec, perm_spec,
        ],
        out_specs=[
            pl.BlockSpec((tm, tn), lambda i, b, c: (b * half_tiles + i, c)),
            pl.BlockSpec((tm, tn), lambda i, b, c: (b * half_tiles + half_tiles - 1 - i, c)),
            head_cols,
        ],
        out_shape=[half, half, jax.ShapeDtypeStruct((n, LRU_WIDTH), BF16)],
        scratch_shapes=[
            pltpu.VMEM((_SEGMENTS * (seq // _SEGMENTS + _SEG_PAD), LRU_BLOCK), F32),
            head_f32, head_f32, head_f32, head_f32,
        ],
        compiler_params=_params("parallel", "parallel", "parallel"),
        name="pos_dft_rglru",
    )(cmat, smat_neg, z16, a_s, sel, rin, conv_w, conv_b, wg, bg, cq, z16, h0, perm, perm.T)


def _tdft_multi_kernel(c_ref, s_ref, ac_ref, as_ref, o_ref, *, nb, seq):
    for b in range(nb):
        r = slice(b * seq, (b + 1) * seq)
        o_ref[r, :] = (jnp.dot(c_ref[...], ac_ref[r, :], preferred_element_type=F32)
                       + jnp.dot(s_ref[...], as_ref[r, :], preferred_element_type=F32)).astype(BF16)


def _tdft_short_tables(seq):
    k = jnp.arange(seq, dtype=jnp.int32)
    ang = ((k[:, None] * k[None, :]) % seq).astype(F32) * (2.0 * math.pi / seq)
    scale = 1.0 / math.sqrt(seq)
    return (jnp.cos(ang) * scale).astype(BF16), (jnp.sin(ang) * (-scale)).astype(BF16)


def _tdft_short(z16, a_s, tables, batch, seq, nb):
    cmat, smat_neg = tables
    n = batch * seq
    tn = COL_BLOCK
    out_shape = jax.ShapeDtypeStruct((n, FOURIER_WIDTH), BF16)
    return pl.pallas_call(
        functools.partial(_tdft_multi_kernel, nb=nb, seq=seq),
        grid=(batch // nb, FOURIER_WIDTH // tn),
        in_specs=[
            pl.BlockSpec((seq, seq), lambda b, c: (0, 0)),
            pl.BlockSpec((seq, seq), lambda b, c: (0, 0)),
            pl.BlockSpec((nb * seq, tn), lambda b, c: (b, SLOT_AC + c)),
            pl.BlockSpec((nb * seq, tn), lambda b, c: (b, c)),
        ],
        out_specs=pl.BlockSpec((nb * seq, tn), lambda b, c: (b, c)),
        out_shape=out_shape,
        compiler_params=_params("parallel", "parallel"),
        name="pos_dft_small",
    )(cmat, smat_neg, z16, a_s)


def _fmix_kernel(flo_ref, fhi_ref, sf_ref, wm_ref, o_ref, *, half_tiles):
    f = flo_ref[...]
    if half_tiles is not None:
        in_low_half = (pl.program_id(0) % (2 * half_tiles)) < half_tiles
        f = jnp.where(in_low_half, f, fhi_ref[...])
    mix = jnp.dot(f, wm_ref[...], preferred_element_type=F32)
    o_ref[...] = (mix * sf_ref[...].astype(F32)).astype(BF16)


def _fmix(f_lo, f_hi, z16, w_fmix, layer, n, tm, half_tiles):
    fw = FOURIER_WIDTH
    if half_tiles is None:
        lo_map = lambda i: (i, 0)
        hi_map = lambda i: (0, 0)
    else:
        def lo_map(i):
            b, r = i // (2 * half_tiles), i % (2 * half_tiles)
            return (b * half_tiles + jnp.minimum(r, half_tiles - 1), 0)

        def hi_map(i):
            b, r = i // (2 * half_tiles), i % (2 * half_tiles)
            return (b * half_tiles + jnp.maximum(r - half_tiles, 0), 0)
    return pl.pallas_call(
        functools.partial(_fmix_kernel, half_tiles=half_tiles),
        grid=(n // tm,),
        in_specs=[
            pl.BlockSpec((tm, fw), lo_map),
            pl.BlockSpec((tm, fw), hi_map),
            pl.BlockSpec((tm, fw), lambda i: (i, SLOT_SF * COL_BLOCK // fw)),
            pl.BlockSpec((None, fw, fw), lambda i: (layer, 0, 0)),
        ],
        out_specs=pl.BlockSpec((tm, fw), lambda i: (i, 0)),
        out_shape=jax.ShapeDtypeStruct((n, fw), BF16),
        compiler_params=_params("parallel"),
        name="fourier_mix",
    )(f_lo, f_hi, z16, w_fmix)


def _merge_kernel(yf_ref, yr_ref, wf_ref, wr_ref, gf_ref, gr_ref, o_ref):
    pf = jnp.dot(yf_ref[...], wf_ref[...], preferred_element_type=F32)
    pr = jnp.dot(yr_ref[...], wr_ref[...], preferred_element_type=F32)
    o_ref[...] = (gf_ref[...].astype(F32) * pf + gr_ref[...].astype(F32) * pr).astype(BF16)


def _merge(yf, yr, z16, w_bf, w_br, layer, tm):
    n = yr.shape[0]
    tn = COL_BLOCK
    return pl.pallas_call(
        _merge_kernel,
        grid=(n // tm, D_MODEL // tn),
        in_specs=[
            pl.BlockSpec((tm, FOURIER_WIDTH), lambda i, j: (i, 0)),
            pl.BlockSpec((tm, LRU_WIDTH), lambda i, j: (i, 0)),
            pl.BlockSpec((None, FOURIER_WIDTH, tn), lambda i, j: (layer, 0, j)),
            pl.BlockSpec((None, LRU_WIDTH, tn), lambda i, j: (layer, 0, j)),
            pl.BlockSpec((tm, tn), lambda i, j: (i, SLOT_SGF + j)),
            pl.BlockSpec((tm, tn), lambda i, j: (i, SLOT_SGR + j)),
        ],
        out_specs=pl.BlockSpec((tm, tn), lambda i, j: (i, j)),
        out_shape=jax.ShapeDtypeStruct((n, D_MODEL), BF16),
        compiler_params=_params("parallel", "parallel"),
        name="branch_merge",
    )(yf, yr, w_bf, w_br, z16, z16)


_OUT_COLS = D_MODEL // COL_BLOCK


def _row_norm(xrow, ng_ref):
    ssq = None
    for c in range(_OUT_COLS):
        v = xrow[c]
        part = jnp.sum(v * v, axis=-1, keepdims=True)
        ssq = part if ssq is None else ssq + part
    inv = lax.rsqrt(ssq * (1.0 / D_MODEL) + EPS)
    for c in range(_OUT_COLS):
        cols = slice(c * COL_BLOCK, (c + 1) * COL_BLOCK)
        yield cols, (xrow[c] * inv) * ng_ref[:, cols]


def _outproj_mid_kernel(m_ref, w_ref, x_ref, g_ref, ng_ref, sc_ref, sh_ref, xo_ref, h_ref, xrow):
    j = pl.program_id(1)
    xn = x_ref[...] + g_ref[...] * jnp.dot(m_ref[...], w_ref[...], preferred_element_type=F32)
    xo_ref[...] = xn
    xrow[j] = xn

    @pl.when(j == _OUT_COLS - 1)
    def _():
        for cols, y in _row_norm(xrow, ng_ref):
            h_ref[:, cols] = (y * (1.0 + sc_ref[:, cols]) + sh_ref[:, cols]).astype(BF16)


def _outproj_last_kernel(m_ref, w_ref, x_ref, g_ref, ng_ref, y_ref, xrow):
    j = pl.program_id(1)
    xrow[j] = x_ref[...] + g_ref[...] * jnp.dot(m_ref[...], w_ref[...], preferred_element_type=F32)

    @pl.when(j == _OUT_COLS - 1)
    def _():
        for cols, y in _row_norm(xrow, ng_ref):
            y_ref[:, cols] = y


def _outproj(merged, w_out, layer, x, gate, row_of, tm, norm_g, scale=None, shift=None):
    n = x.shape[0]
    tn = COL_BLOCK
    last = scale is None
    row_block = pl.BlockSpec((tm, D_MODEL), lambda i, j: (i, 0))
    col_block = pl.BlockSpec((tm, tn), lambda i, j: (i, j))
    mod_row = pl.BlockSpec((None, 1, D_MODEL), lambda i, j: (row_of(i), 0, 0))
    in_specs = [
        row_block,
        pl.BlockSpec((None, D_MODEL, tn), lambda i, j: (layer, 0, j)),
        col_block,
        pl.BlockSpec((None, 1, tn), lambda i, j: (row_of(i), 0, j)),
        pl.BlockSpec((1, D_MODEL), lambda i, j: (0, 0)),
    ]
    scratch = [pltpu.VMEM((_OUT_COLS, tm, tn), F32)]
    if last:
        return pl.pallas_call(
            _outproj_last_kernel,
            grid=(n // tm, _OUT_COLS),
            in_specs=in_specs,
            out_specs=row_block,
            out_shape=jax.ShapeDtypeStruct((n, D_MODEL), F32),
            scratch_shapes=scratch,
            compiler_params=_params("parallel", "arbitrary"),
            name="out_proj_final_norm",
        )(merged, w_out, x, gate, norm_g)
    return pl.pallas_call(
        _outproj_mid_kernel,
        grid=(n // tm, _OUT_COLS),
        in_specs=in_specs + [mod_row, mod_row],
        out_specs=[col_block, row_block],
        out_shape=[jax.ShapeDtypeStruct((n, D_MODEL), F32),
                   jax.ShapeDtypeStruct((n, D_MODEL), BF16)],
        scratch_shapes=scratch,
        compiler_params=_params("parallel", "arbitrary"),
        name="out_proj_norm",
    )(merged, w_out, x, gate, norm_g, scale, shift)


class _Tiles(NamedTuple):
    norm: int
    inproj: int
    merge: int
    outproj: int
    lru_channels: Optional[int]
    dft_rows: Optional[int]
    dft_batch: Optional[int]


_LONG_SEQ = 1024


def _tiles(seq):
    if seq >= _LONG_SEQ:
        return _Tiles(norm=512, inproj=1024, merge=1024, outproj=512, lru_channels=None,
                      dft_rows=512, dft_batch=None)
    return _Tiles(norm=512, inproj=1024, merge=1024, outproj=512, lru_channels=LRU_WIDTH,
                  dft_rows=None, dft_batch=8)


def _channel_dft_table():
    k = jnp.arange(FOURIER_GROUP_CH, dtype=jnp.int32)
    ang = ((k[:, None] * k[None, :]) % FOURIER_GROUP_CH).astype(F32) * (2.0 * math.pi / FOURIER_GROUP_CH)
    scale = 1.0 / math.sqrt(FOURIER_GROUP_CH)
    return jnp.concatenate([jnp.cos(ang) * scale, jnp.sin(ang) * scale], axis=1).astype(BF16)


def _per_head(v, parts):
    return (v.reshape(parts, LRU_HEADS, LRU_BLOCK).transpose(1, 0, 2)
            .reshape(LRU_HEADS, 1, parts * LRU_BLOCK))


def _stream(x, batch, seq, row_of_token_tile, h0_layers, mods, big, small, cs, final_g):
    n = batch * seq
    x = x.reshape(n, D_MODEL)
    tiles = _tiles(seq)
    long_seq = tiles.dft_rows is not None
    if long_seq:
        dft_tables = _tdft_sym_tables(seq, tiles.dft_rows)
        half_tiles = seq // 2 // tiles.merge
    else:
        dft_tables = _tdft_short_tables(seq)
        half_tiles = None
    states = []
    shift, scale, _ = mods[0]
    h = _norm_mod(x, small[0]["norm_g"], scale, shift, row_of_token_tile(tiles.norm), tiles.norm)
    for l in range(DEPTH):
        w = small[l]
        gate = mods[l][2]
        z16, a_s, rin = _inproj(h, big["w_in"], l, cs, tiles.inproj)
        if long_seq:
            f_lo, f_hi, yr = _tdft_lru(
                z16, a_s, rin, dft_tables, w["conv_w"], w["conv_b"], w["wg"], w["bg"], w["cq"],
                h0_layers[l], batch, seq, tiles.dft_rows)
        else:
            f_lo = f_hi = _tdft_short(z16, a_s, dft_tables, batch, seq, tiles.dft_batch)
            yr, st = _lru(rin, z16, w["conv_w"], w["conv_b"], w["wg"], w["bg"], w["cq"],
                          h0_layers[l], batch, seq, tiles.lru_channels)
            states.append(st)
        yf = _fmix(f_lo, f_hi, z16, big["w_fmix"], l, n, tiles.merge, half_tiles)
        merged = _merge(yf, yr, z16, big["w_bf"], big["w_br"], l, tiles.merge)
        row_of = row_of_token_tile(tiles.outproj)
        if l + 1 < DEPTH:
            next_shift, next_scale, _ = mods[l + 1]
            x, h = _outproj(merged, big["w_out"], l, x, gate, row_of, tiles.outproj,
                            small[l + 1]["norm_g"], next_scale, next_shift)
        else:
            y = _outproj(merged, big["w_out"], l, x, gate, row_of, tiles.outproj, final_g)
    return y.reshape(batch, seq, D_MODEL), states


def kernel(x_prompt, x_sample, state_lru, c, c_ctx, norm_g, w_ada, b_ada, w_in, w_fmix,
           conv_w, conv_b, w_gate_a, b_gate_a, w_gate_x, b_gate_x, lam,
           w_branch_f, w_branch_r, w_out, final_g):
    batch, seq, _ = x_prompt.shape
    dec_batch, dec_seq, _ = x_sample.shape
    assert 1 + dec_batch <= COND_ROWS

    cond = jnp.zeros((COND_ROWS, D_MODEL), F32)
    cond = cond.at[0].set(c_ctx).at[1:1 + dec_batch].set(c)
    mod = _ada(cond, w_ada, b_ada)
    mods = []
    for l in range(DEPTH):
        parts = jnp.split(mod[l], 3, axis=-1)
        mods.append(tuple(p.reshape(COND_ROWS, 1, D_MODEL) for p in parts))

    big = dict(w_in=w_in.astype(BF16), w_fmix=w_fmix.astype(BF16), w_bf=w_branch_f.astype(BF16),
               w_br=w_branch_r.astype(BF16), w_out=w_out.astype(BF16))
    small = []
    for l in range(DEPTH):
        wg = (0.5 * jnp.concatenate([w_gate_a[l, 0], w_gate_x[l, 0], w_gate_a[l, 1], w_gate_x[l, 1]],
                                    axis=-1)).astype(BF16)
        bg = _per_head(0.5 * jnp.stack([b_gate_a[l, 0], b_gate_x[l, 0], b_gate_a[l, 1], b_gate_x[l, 1]]), 4)
        cq = _per_head((0.25 * LRU_C) * jax.nn.softplus(-lam[l]), 2)
        small.append(dict(norm_g=norm_g[l].reshape(1, D_MODEL), conv_w=conv_w[l],
                          conv_b=conv_b[l].reshape(1, LRU_WIDTH), wg=wg, bg=bg, cq=cq))

    cs = _channel_dft_table()
    final_g = final_g.reshape(1, D_MODEL)

    zeros_h0 = jnp.zeros((batch, 2, LRU_WIDTH), F32)
    y_prompt, ctx_states = _stream(
        x_prompt, batch, seq, lambda tm: (lambda i: 0), [zeros_h0] * DEPTH,
        mods, big, small, cs, final_g)
    new_state = jnp.stack(ctx_states, axis=1)

    y_sample, _ = _stream(
        x_sample, dec_batch, dec_seq, lambda tm: (lambda i: 1 + (i * tm) // dec_seq),
        [state_lru[:, l] for l in range(DEPTH)], mods, big, small, cs, final_g)
    return (y_prompt, y_sample, new_state)
```

```python
import functools
import math
from typing import NamedTuple, Optional

import jax
import jax.numpy as jnp
from jax import lax
from jax.experimental import pallas as pl
from jax.experimental.pallas import tpu as pltpu

F32 = jnp.float32
BF16 = jnp.bfloat16

D_MODEL = 4096
DEPTH = 2
FOURIER_WIDTH = D_MODEL // 2
FOURIER_GROUP_CH = 256
LRU_WIDTH = D_MODEL // 2
LRU_BLOCK = 128
LRU_HEADS = LRU_WIDTH // LRU_BLOCK
CONV_WIDTH = 4
LRU_C = 8.0
IN_WIDTH = 2 * FOURIER_WIDTH + 2 * LRU_WIDTH + 2 * D_MODEL
EPS = 1e-6

SUBLANES = 8
LANES = 128
VMEM_LIMIT_BYTES = 58 * 1024 * 1024
F32_MIN_NORMAL = 2.0 ** -126

COND_ROWS = 16
COL_BLOCK = 1024
SLOT_AC, SLOT_SF, SLOT_SR, SLOT_SGF, SLOT_SGR = 0, 2, 4, 6, 10
PACKED_BLOCKS = 14


def _sigmoid(x):
    return 0.5 * jnp.tanh(0.5 * x) + 0.5


def _params(*sem):
    return pltpu.CompilerParams(dimension_semantics=sem, vmem_limit_bytes=VMEM_LIMIT_BYTES)


def _ada_kernel(c_ref, w_ref, b_ref, o_ref):
    c = c_ref[...]
    s = (c * _sigmoid(c)).astype(BF16)
    w = w_ref[...].astype(BF16)
    o_ref[...] = jnp.dot(s, w, preferred_element_type=F32) + b_ref[...]


def _ada(cond, w_ada, b_ada):
    tn = 512
    n_out = 3 * D_MODEL
    return pl.pallas_call(
        _ada_kernel,
        grid=(DEPTH, n_out // tn),
        in_specs=[
            pl.BlockSpec((COND_ROWS, D_MODEL), lambda l, j: (0, 0)),
            pl.BlockSpec((None, D_MODEL, tn), lambda l, j: (l, 0, j)),
            pl.BlockSpec((None, 1, tn), lambda l, j: (l, 0, j)),
        ],
        out_specs=pl.BlockSpec((None, COND_ROWS, tn), lambda l, j: (l, 0, j)),
        out_shape=jax.ShapeDtypeStruct((DEPTH, COND_ROWS, n_out), F32),
        compiler_params=_params("parallel", "parallel"),
        name="ada_mod",
    )(cond, w_ada, b_ada.reshape(DEPTH, 1, n_out))


def _norm_mod_kernel(x_ref, g_ref, sc_ref, sh_ref, o_ref):
    x = x_ref[...]
    ms = jnp.mean(x * x, axis=-1, keepdims=True)
    y = (x * lax.rsqrt(ms + EPS)) * g_ref[...]
    o_ref[...] = (y * (1.0 + sc_ref[...]) + sh_ref[...]).astype(o_ref.dtype)


def _norm_mod(x, g, scale, shift, row_of, tm):
    n = x.shape[0]
    return pl.pallas_call(
        _norm_mod_kernel,
        grid=(n // tm,),
        in_specs=[
            pl.BlockSpec((tm, D_MODEL), lambda i: (i, 0)),
            pl.BlockSpec((1, D_MODEL), lambda i: (0, 0)),
            pl.BlockSpec((None, 1, D_MODEL), lambda i: (row_of(i), 0, 0)),
            pl.BlockSpec((None, 1, D_MODEL), lambda i: (row_of(i), 0, 0)),
        ],
        out_specs=pl.BlockSpec((tm, D_MODEL), lambda i: (i, 0)),
        out_shape=jax.ShapeDtypeStruct((n, D_MODEL), BF16),
        compiler_params=_params("parallel"),
        name="norm_mod",
    )(x, g, scale, shift)


def _inproj_kernel(h_ref, w_ref, cs_ref, z16_ref, as_ref, rin_ref):
    j = pl.program_id(1)

    def z():
        return jnp.dot(h_ref[...], w_ref[...], preferred_element_type=F32)

    @pl.when(j < 2)
    def _():
        zb = z().astype(BF16)
        for g in range(COL_BLOCK // FOURIER_GROUP_CH):
            cols = slice(g * FOURIER_GROUP_CH, (g + 1) * FOURIER_GROUP_CH)
            t = jnp.dot(zb[:, cols], cs_ref[...], preferred_element_type=F32)
            z16_ref[:, cols] = t[:, :FOURIER_GROUP_CH].astype(BF16)
            as_ref[:, cols] = t[:, FOURIER_GROUP_CH:].astype(BF16)

    @pl.when(((j >= 2) & (j < 4)) | ((j >= 6) & (j < 8)))
    def _():
        v = z()
        z16_ref[...] = (v * _sigmoid(v)).astype(BF16)

    @pl.when((j >= 4) & (j < 6))
    def _():
        rin_ref[...] = z()

    @pl.when(j >= 8)
    def _():
        z16_ref[...] = _sigmoid(z()).astype(BF16)


def _packed_slot(j):
    return jnp.where(j < 4, j, jnp.where(j < 6, 3, j - 2))


def _inproj(h, w_in, layer, cs, tm):
    n = h.shape[0]
    tn = COL_BLOCK
    return pl.pallas_call(
        _inproj_kernel,
        grid=(n // tm, IN_WIDTH // tn),
        in_specs=[
            pl.BlockSpec((tm, D_MODEL), lambda i, j: (i, 0)),
            pl.BlockSpec((None, D_MODEL, tn), lambda i, j: (layer, 0, j)),
            pl.BlockSpec((FOURIER_GROUP_CH, 2 * FOURIER_GROUP_CH), lambda i, j: (0, 0)),
        ],
        out_specs=[
            pl.BlockSpec((tm, tn), lambda i, j: (i, _packed_slot(j))),
            pl.BlockSpec((tm, tn), lambda i, j: (i, jnp.minimum(j, 1))),
            pl.BlockSpec((tm, tn), lambda i, j: (i, jnp.clip(j - 4, 0, 1))),
        ],
        out_shape=[
            jax.ShapeDtypeStruct((n, PACKED_BLOCKS * COL_BLOCK), BF16),
            jax.ShapeDtypeStruct((n, FOURIER_WIDTH), BF16),
            jax.ShapeDtypeStruct((n, LRU_WIDTH), F32),
        ],
        compiler_params=_params("parallel", "arbitrary"),
        name="in_proj",
    )(h, w_in, cs)


_HALO = SUBLANES
_GATE_ROWS = 256
_SEGMENTS = SUBLANES
_SEG_PAD = SUBLANES


def _conv_taps(xe, cw_ref, cb_ref, rows):
    ext = rows + 2 * _HALO
    inner = slice(_HALO, _HALO + rows)
    xc = cb_ref[...] + pltpu.roll(xe, 1, 0)[inner] * cw_ref[0:1, :]
    xc = xc + xe[inner] * cw_ref[1:2, :]
    xc = xc + pltpu.roll(xe, ext - 1, 0)[inner] * cw_ref[2:3, :]
    return xc + pltpu.roll(xe, ext - 2, 0)[inner] * cw_ref[3:4, :]


def _gate_terms(gh, xh, cq):
    out = []
    for d in range(2):
        base = 2 * d * LRU_BLOCK
        t_a = jnp.tanh(gh[:, base:base + LRU_BLOCK])
        t_x = jnp.tanh(gh[:, base + LRU_BLOCK:base + 2 * LRU_BLOCK])
        c = cq[:, d * LRU_BLOCK:(d + 1) * LRU_BLOCK]
        s = jnp.tanh(c * t_a + c)
        inv = 1.0 / (1.0 + s)
        root = s * lax.rsqrt(jnp.maximum(s, F32_MIN_NORMAL))
        a = (1.0 - s) * inv
        u = (inv * root) * (xh * t_x + xh)
        out.append((a, u))
    return out


def _block_scan(a, b, row, reverse):
    for s in (1, 2, 4):
        if reverse:
            m, sh = row < SUBLANES - s, SUBLANES - s
        else:
            m, sh = row >= s, s
        a_sh = jnp.where(m, pltpu.roll(a, sh, 0), 1.0)
        b_sh = jnp.where(m, pltpu.roll(b, sh, 0), 0.0)
        b = b + a * b_sh
        a = a * a_sh
    return a, b


_SCAN_ROWS = 2 * SUBLANES


def _scan_both(a_f, u_f, a_b, u_b, h_f, h_b, h0_ref, st_ref, *, seq, tc, cs):
    row = lax.broadcasted_iota(jnp.int32, (SUBLANES, cs), 0)
    for c0 in range(0, tc, cs):
        cols = slice(c0, c0 + cs)

        def body(k, carry, cols=cols):
            carry_f, carry_b = carry
            rf = pl.multiple_of(k * _SCAN_ROWS, _SCAN_ROWS)
            rb = pl.multiple_of(seq - _SCAN_ROWS - k * _SCAN_ROWS, _SCAN_ROWS)
            a2 = a_f[pl.ds(rf, _SCAN_ROWS), cols]
            b2 = u_f[pl.ds(rf, _SCAN_ROWS), cols].astype(F32)
            hs = []
            for p in (0, 1):
                blk = slice(p * SUBLANES, (p + 1) * SUBLANES)
                a, b = _block_scan(a2[blk], b2[blk], row, False)
                h = a * carry_f + b
                hs.append(h)
                carry_f = jnp.broadcast_to(h[SUBLANES - 1:SUBLANES, :], (SUBLANES, cs))
            h_f[pl.ds(rf, _SCAN_ROWS), cols] = jnp.concatenate(hs, axis=0)

            a2 = a_b[pl.ds(rb, _SCAN_ROWS), cols]
            b2 = u_b[pl.ds(rb, _SCAN_ROWS), cols].astype(F32)
            hs = [None, None]
            for p in (1, 0):
                blk = slice(p * SUBLANES, (p + 1) * SUBLANES)
                a, b = _block_scan(a2[blk], b2[blk], row, True)
                h = a * carry_b + b
                hs[p] = h
                carry_b = jnp.broadcast_to(h[0:1, :], (SUBLANES, cs))
            h_b[pl.ds(rb, _SCAN_ROWS), cols] = jnp.concatenate(hs, axis=0)
            return carry_f, carry_b

        init = (jnp.broadcast_to(h0_ref[0:1, cols], (SUBLANES, cs)),
                jnp.broadcast_to(h0_ref[1:2, cols], (SUBLANES, cs)))
        fin_f, fin_b = lax.fori_loop(0, seq // _SCAN_ROWS, body, init)
        st_ref[0:1, cols] = fin_f[0:1, :]
        st_ref[1:2, cols] = fin_b[0:1, :]


def _write_gated_sum(h_f, h_b, sr_ref, y_ref, seq):
    def body(c, carry):
        r = pl.ds(pl.multiple_of(c * _GATE_ROWS, _GATE_ROWS), _GATE_ROWS)
        y_ref[r, :] = ((h_f[r, :] + h_b[r, :]) * sr_ref[r, :].astype(F32)).astype(BF16)
        return carry
    lax.fori_loop(0, seq // _GATE_ROWS, body, 0)


def _lru_kernel(rin_ref, sr_ref, cw_ref, cb_ref, wg_ref, bg_ref, cq_ref, h0_ref,
                y_ref, st_ref, xpad, a_f, u_f, a_b, u_b, *, seq, tc, cs):
    heads = tc // LRU_BLOCK
    n_chunks = seq // _GATE_ROWS

    def rows(c):
        return pl.multiple_of(c * _GATE_ROWS, _GATE_ROWS)

    zero_halo = jnp.zeros((_HALO, tc), F32)
    xpad[0:_HALO, :] = zero_halo
    xpad[seq + _HALO:seq + 2 * _HALO, :] = zero_halo

    def copy_body(c, carry):
        r0 = rows(c)
        xpad[pl.ds(r0 + _HALO, _GATE_ROWS), :] = rin_ref[pl.ds(r0, _GATE_ROWS), :]
        return carry
    lax.fori_loop(0, n_chunks, copy_body, 0)

    def gate_body(c, carry):
        r0 = rows(c)
        xc = _conv_taps(xpad[pl.ds(r0, _GATE_ROWS + 2 * _HALO), :], cw_ref, cb_ref, _GATE_ROWS)
        for hh in range(heads):
            cols = slice(hh * LRU_BLOCK, (hh + 1) * LRU_BLOCK)
            xh = xc[:, cols]
            gh = jnp.dot(xh.astype(BF16), wg_ref[hh], preferred_element_type=F32) + bg_ref[hh]
            (af, uf), (ab, ub) = _gate_terms(gh, xh, cq_ref[hh])
            a_f[pl.ds(r0, _GATE_ROWS), cols] = af
            u_f[pl.ds(r0, _GATE_ROWS), cols] = uf
            a_b[pl.ds(r0, _GATE_ROWS), cols] = ab
            u_b[pl.ds(r0, _GATE_ROWS), cols] = ub
        return carry
    lax.fori_loop(0, n_chunks, gate_body, 0)

    _scan_both(a_f, u_f, a_b, u_b, a_f, a_b, h0_ref, st_ref, seq=seq, tc=tc, cs=cs)
    _write_gated_sum(a_f, a_b, sr_ref, y_ref, seq)


def _lru(rin, z16, conv_w, conv_b, wg, bg, cq, h0, batch, seq, tc):
    n = batch * seq
    heads = tc // LRU_BLOCK
    cs = min(tc, 512)
    sr_block0 = SLOT_SR * COL_BLOCK // tc
    kern = functools.partial(_lru_kernel, seq=seq, tc=tc, cs=cs)
    return pl.pallas_call(
        kern,
        grid=(batch, LRU_WIDTH // tc),
        in_specs=[
            pl.BlockSpec((seq, tc), lambda b, c: (b, c)),
            pl.BlockSpec((seq, tc), lambda b, c: (b, sr_block0 + c)),
            pl.BlockSpec((CONV_WIDTH, tc), lambda b, c: (0, c)),
            pl.BlockSpec((1, tc), lambda b, c: (0, c)),
            pl.BlockSpec((heads, LRU_BLOCK, 4 * LRU_BLOCK), lambda b, c: (c, 0, 0)),
            pl.BlockSpec((heads, 1, 4 * LRU_BLOCK), lambda b, c: (c, 0, 0)),
            pl.BlockSpec((heads, 1, 2 * LRU_BLOCK), lambda b, c: (c, 0, 0)),
            pl.BlockSpec((None, 2, tc), lambda b, c: (b, 0, c)),
        ],
        out_specs=[
            pl.BlockSpec((seq, tc), lambda b, c: (b, c)),
            pl.BlockSpec((None, 2, tc), lambda b, c: (b, 0, c)),
        ],
        out_shape=[
            jax.ShapeDtypeStruct((n, LRU_WIDTH), BF16),
            jax.ShapeDtypeStruct((batch, 2, LRU_WIDTH), F32),
        ],
        scratch_shapes=[
            pltpu.VMEM((seq + 2 * _HALO, tc), F32),
            pltpu.VMEM((seq, tc), F32),
            pltpu.VMEM((seq, tc), F32),
            pltpu.VMEM((seq, tc), F32),
            pltpu.VMEM((seq, tc), F32),
        ],
        compiler_params=_params("parallel", "parallel"),
        name="conv_rglru",
    )(rin, z16, conv_w, conv_b, wg, bg, cq, h0)


def _segment_permutation():
    per_chunk = _GATE_ROWS // _SEGMENTS
    r = jnp.arange(_GATE_ROWS, dtype=jnp.int32)
    src = (r % _SEGMENTS) * per_chunk + r // _SEGMENTS
    return (jnp.arange(_GATE_ROWS, dtype=jnp.int32)[None, :] == src[:, None]).astype(BF16)


_MIRROR_PAD = 16


def _tdft_lru_kernel(c_ref, s_ref, ac_ref, as_ref, sel_ref, rin_ref, cw_ref, cb_ref, wg_ref,
                     bg_ref, cq_ref, sr_ref, h0_ref, pm_ref, pmt_ref, lo_ref, hi_ref, y_ref,
                     xs, a_f, u_f, a_b, u_b, *, tm, seq):
    seg = seq // _SEGMENTS
    pitch = seg + _SEG_PAD
    for s in range(_SEGMENTS):
        xs[s * pitch:s * pitch + seg, :] = rin_ref[s * seg:(s + 1) * seg, :]
    row = lax.broadcasted_iota(jnp.int32, (_SEGMENTS, LRU_BLOCK), 0)

    def positions(j):
        if 0 <= j < seg:
            return xs[pl.ds(j, _SEGMENTS, stride=pitch), :]
        if j < 0:
            v = xs[pl.ds(seg + j, _SEGMENTS, stride=pitch), :]
            return jnp.where(row >= 1, pltpu.roll(v, 1, 0), 0.0)
        v = xs[pl.ds(j - seg, _SEGMENTS, stride=pitch), :]
        return jnp.where(row < _SEGMENTS - 1, pltpu.roll(v, _SEGMENTS - 1, 0), 0.0)

    taps = [jnp.broadcast_to(cw_ref[i:i + 1, :], (_SEGMENTS, LRU_BLOCK)) for i in range(CONV_WIDTH)]
    bias = jnp.broadcast_to(cb_ref[...], (_SEGMENTS, LRU_BLOCK))
    per_chunk = _GATE_ROWS // _SEGMENTS

    n_chunks = seq // _GATE_ROWS
    blocks = [slice(jj * _SEGMENTS, (jj + 1) * _SEGMENTS) for jj in range(per_chunk)]

    def chunk_rows(k):
        return slice(k * _GATE_ROWS, (k + 1) * _GATE_ROWS)

    def chunk_slabs(k):
        return [slice(s * seg + k * per_chunk, s * seg + (k + 1) * per_chunk)
                for s in range(_SEGMENTS)]

    def dft_slice(k, acc):
        r = chunk_rows(k)
        dp = jnp.dot(c_ref[:, r], ac_ref[r, :], preferred_element_type=F32)
        dq = jnp.dot(s_ref[:, r], as_ref[r, :], preferred_element_type=F32)
        return (dp, dq) if acc is None else (acc[0] + dp, acc[1] + dq)

    zeros = jnp.zeros((_SEGMENTS, LRU_BLOCK), F32)
    ones = jnp.ones((_SEGMENTS, LRU_BLOCK), F32)
    end_f, prod_f, end_b, prod_b = zeros, ones, zeros, ones
    acc = None
    for k in range(n_chunks):
        r = chunk_rows(k)
        j0 = k * per_chunk
        vs = [positions(j) for j in range(j0 - 1, j0 + per_chunk + CONV_WIDTH - 2)]
        xc = jnp.concatenate(
            [((bias + vs[jj] * taps[0]) + vs[jj + 1] * taps[1] + vs[jj + 2] * taps[2])
             + vs[jj + 3] * taps[3] for jj in range(per_chunk)], axis=0)
        gh = jnp.dot(xc.astype(BF16), wg_ref[...], preferred_element_type=F32) + bg_ref[...]
        acc = dft_slice(k, acc)

        (af, uf), (ab, ub) = _gate_terms(gh, xc, cq_ref[...])
        a_f[r, :] = af
        u_f[r, :] = uf
        a_b[r, :] = ab
        u_b[r, :] = ub
        for blk in blocks:
            end_f = af[blk] * end_f + uf[blk]
            prod_f = prod_f * af[blk]
            end_b = end_b + prod_b * ub[blk]
            prod_b = prod_b * ab[blk]

    p, q = acc
    lo_ref[...] = (p[:tm] + q[:tm]).astype(BF16)
    mirrored = (p - q).astype(BF16)
    hi_ref[...] = jnp.dot(sel_ref[...], mirrored, preferred_element_type=F32).astype(BF16)

    h0_f = jnp.broadcast_to(h0_ref[0:1, :], (_SEGMENTS, LRU_BLOCK))
    h0_b = jnp.broadcast_to(h0_ref[1:2, :], (_SEGMENTS, LRU_BLOCK))
    a_cum, b_cum = _block_scan(prod_f, end_f, row, False)
    true_end_f = a_cum * h0_f + b_cum
    h_fwd = jnp.where(row >= 1, pltpu.roll(true_end_f, 1, 0), h0_f)
    a_cum, b_cum = _block_scan(prod_b, end_b, row, True)
    true_end_b = a_cum * h0_b + b_cum
    h_bwd = jnp.where(row < _SEGMENTS - 1, pltpu.roll(true_end_b, _SEGMENTS - 1, 0), h0_b)

    def park_sr(k):
        sr_pos = jnp.concatenate([sr_ref[sl, :] for sl in chunk_slabs(k)], axis=0)
        xs[chunk_rows(k), :] = jnp.dot(pm_ref[...], sr_pos, preferred_element_type=F32)

    def emit_y(k):
        r = chunk_rows(k)
        y_seg = ((u_f[r, :] + u_b[r, :]) * xs[r, :]).astype(BF16)
        y_pos = jnp.dot(pmt_ref[...], y_seg, preferred_element_type=F32).astype(BF16)
        for s, sl in enumerate(chunk_slabs(k)):
            y_ref[sl, :] = y_pos[s * per_chunk:(s + 1) * per_chunk, :]

    half_chunks = n_chunks // 2
    for k in range(n_chunks):
        if k < half_chunks:
            park_sr(k)
            park_sr(n_chunks - 1 - k)
        rf, rb = chunk_rows(k), chunk_rows(n_chunks - 1 - k)
        af_c, uf_c, ab_c, ub_c = a_f[rf, :], u_f[rf, :], a_b[rb, :], u_b[rb, :]
        hfs, hbs = [], [None] * per_chunk
        for jj in range(per_chunk):
            h_fwd = af_c[blocks[jj]] * h_fwd + uf_c[blocks[jj]]
            hfs.append(h_fwd)
            jb = per_chunk - 1 - jj
            h_bwd = ab_c[blocks[jb]] * h_bwd + ub_c[blocks[jb]]
            hbs[jb] = h_bwd
        u_f[rf, :] = jnp.concatenate(hfs, axis=0)
        u_b[rb, :] = jnp.concatenate(hbs, axis=0)
        if k >= half_chunks:
            emit_y(k)
            emit_y(n_chunks - 1 - k)


def _tdft_sym_tables(seq, tm):
    half_tiles = seq // 2 // tm
    ext = tm + _MIRROR_PAD
    k = (jnp.arange(half_tiles, dtype=jnp.int32)[:, None] * tm
         + jnp.arange(ext, dtype=jnp.int32)[None, :]).reshape(-1)
    side = math.isqrt(seq)
    assert side * side == seq
    tt = jnp.arange(side, dtype=jnp.int32)
    coarse = ((k[:, None] * tt[None, :]) % side).astype(F32) * (2.0 * math.pi / side)
    fine = ((k[:, None] * tt[None, :]) % seq).astype(F32) * (2.0 * math.pi / seq)
    t = jnp.arange(seq, dtype=jnp.int32)
    pick_coarse = (tt[:, None] == (t // side)[None, :]).astype(F32)
    pick_fine = (tt[:, None] == (t % side)[None, :]).astype(F32)
    expand = functools.partial(jnp.dot, precision=lax.Precision.HIGHEST)
    ca, sa = expand(jnp.cos(coarse), pick_coarse), expand(jnp.sin(coarse), pick_coarse)
    cb, sb = expand(jnp.cos(fine), pick_fine), expand(jnp.sin(fine), pick_fine)
    scale = 1.0 / math.sqrt(seq)
    cmat = ((ca * cb - sa * sb) * scale).astype(BF16).reshape(half_tiles, ext, seq)
    smat_neg = ((sa * cb + ca * sb) * (-scale)).astype(BF16).reshape(half_tiles, ext, seq)
    r = jnp.arange(tm, dtype=jnp.int32)
    sel = (jnp.arange(ext, dtype=jnp.int32)[None, :] == (tm - r)[:, None]).astype(BF16)
    return cmat, smat_neg, sel


def _tdft_lru(z16, a_s, rin, tables, conv_w, conv_b, wg, bg, cq, h0, batch, seq, tm):
    cmat, smat_neg, sel = tables
    tn = COL_BLOCK // 2
    ac_block0 = SLOT_AC * COL_BLOCK // tn
    sr_block0 = SLOT_SR * COL_BLOCK // LRU_BLOCK
    perm = _segment_permutation()
    half_tiles = seq // 2 // tm
    col_tiles = FOURIER_WIDTH // tn
    assert col_tiles * half_tiles == LRU_HEADS
    ext = tm + _MIRROR_PAD
    n = batch * seq
    half = jax.ShapeDtypeStruct((n // 2, FOURIER_WIDTH), BF16)

    def head(c, i):
        return c * half_tiles + i

    head_cols = pl.BlockSpec((seq, LRU_BLOCK), lambda i, b, c: (b, head(c, i)))
    perm_spec = pl.BlockSpec((_GATE_ROWS, _GATE_ROWS), lambda i, b, c: (0, 0))
    head_f32 = pltpu.VMEM((seq, LRU_BLOCK), F32)
    return pl.pallas_call(
        functools.partial(_tdft_lru_kernel, tm=tm, seq=seq),
        grid=(half_tiles, batch, col_tiles),
        in_specs=[
            pl.BlockSpec((None, ext, seq), lambda i, b, c: (i, 0, 0)),
            pl.BlockSpec((None, ext, seq), lambda i, b, c: (i, 0, 0)),
            pl.BlockSpec((seq, tn), lambda i, b, c: (b, ac_block0 + c)),
            pl.BlockSpec((seq, tn), lambda i, b, c: (b, c)),
            pl.BlockSpec((tm, ext), lambda i, b, c: (0, 0)),
            head_cols,
            pl.BlockSpec((CONV_WIDTH, LRU_BLOCK), lambda i, b, c: (0, head(c, i))),
            pl.BlockSpec((1, LRU_BLOCK), lambda i, b, c: (0, head(c, i))),
            pl.BlockSpec((None, LRU_BLOCK, 4 * LRU_BLOCK), lambda i, b, c: (head(c, i), 0, 0)),
            pl.BlockSpec((None, 1, 4 * LRU_BLOCK), lambda i, b, c: (head(c, i), 0, 0)),
            pl.BlockSpec((None, 1, 2 * LRU_BLOCK), lambda i, b, c: (head(c, i), 0, 0)),
            pl.BlockSpec((seq, LRU_BLOCK), lambda i, b, c: (b, sr_block0 + head(c, i))),
            pl.BlockSpec((None, 2, LRU_BLOCK), lambda i, b, c: (b, 0, head(c, i))),
            perm_spec, perm_spec,
        ],
        out_specs=[
            pl.BlockSpec((tm, tn), lambda i, b, c: (b * half_tiles + i, c)),
            pl.BlockSpec((tm, tn), lambda i, b, c: (b * half_tiles + half_tiles - 1 - i, c)),
            head_cols,
        ],
        out_shape=[half, half, jax.ShapeDtypeStruct((n, LRU_WIDTH), BF16)],
        scratch_shapes=[
            pltpu.VMEM((_SEGMENTS * (seq // _SEGMENTS + _SEG_PAD), LRU_BLOCK), F32),
            head_f32, head_f32, head_f32, head_f32,
        ],
        compiler_params=_params("parallel", "parallel", "parallel"),
        name="pos_dft_rglru",
    )(cmat, smat_neg, z16, a_s, sel, rin, conv_w, conv_b, wg, bg, cq, z16, h0, perm, perm.T)


def _tdft_multi_kernel(c_ref, s_ref, ac_ref, as_ref, o_ref, *, nb, seq):
    for b in range(nb):
        r = slice(b * seq, (b + 1) * seq)
        o_ref[r, :] = (jnp.dot(c_ref[...], ac_ref[r, :], preferred_element_type=F32)
                       + jnp.dot(s_ref[...], as_ref[r, :], preferred_element_type=F32)).astype(BF16)


def _tdft_short_tables(seq):
    k = jnp.arange(seq, dtype=jnp.int32)
    ang = ((k[:, None] * k[None, :]) % seq).astype(F32) * (2.0 * math.pi / seq)
    scale = 1.0 / math.sqrt(seq)
    return (jnp.cos(ang) * scale).astype(BF16), (jnp.sin(ang) * (-scale)).astype(BF16)


def _tdft_short(z16, a_s, tables, batch, seq, nb):
    cmat, smat_neg = tables
    n = batch * seq
    tn = COL_BLOCK
    out_shape = jax.ShapeDtypeStruct((n, FOURIER_WIDTH), BF16)
    return pl.pallas_call(
        functools.partial(_tdft_multi_kernel, nb=nb, seq=seq),
        grid=(batch // nb, FOURIER_WIDTH // tn),
        in_specs=[
            pl.BlockSpec((seq, seq), lambda b, c: (0, 0)),
            pl.BlockSpec((seq, seq), lambda b, c: (0, 0)),
            pl.BlockSpec((nb * seq, tn), lambda b, c: (b, SLOT_AC + c)),
            pl.BlockSpec((nb * seq, tn), lambda b, c: (b, c)),
        ],
        out_specs=pl.BlockSpec((nb * seq, tn), lambda b, c: (b, c)),
        out_shape=out_shape,
        compiler_params=_params("parallel", "parallel"),
        name="pos_dft_small",
    )(cmat, smat_neg, z16, a_s)


def _fmix_kernel(flo_ref, fhi_ref, sf_ref, wm_ref, o_ref, *, half_tiles):
    f = flo_ref[...]
    if half_tiles is not None:
        in_low_half = (pl.program_id(0) % (2 * half_tiles)) < half_tiles
        f = jnp.where(in_low_half, f, fhi_ref[...])
    mix = jnp.dot(f, wm_ref[...], preferred_element_type=F32)
    o_ref[...] = (mix * sf_ref[...].astype(F32)).astype(BF16)


def _fmix(f_lo, f_hi, z16, w_fmix, layer, n, tm, half_tiles):
    fw = FOURIER_WIDTH
    if half_tiles is None:
        lo_map = lambda i: (i, 0)
        hi_map = lambda i: (0, 0)
    else:
        def lo_map(i):
            b, r = i // (2 * half_tiles), i % (2 * half_tiles)
            return (b * half_tiles + jnp.minimum(r, half_tiles - 1), 0)

        def hi_map(i):
            b, r = i // (2 * half_tiles), i % (2 * half_tiles)
            return (b * half_tiles + jnp.maximum(r - half_tiles, 0), 0)
    return pl.pallas_call(
        functools.partial(_fmix_kernel, half_tiles=half_tiles),
        grid=(n // tm,),
        in_specs=[
            pl.BlockSpec((tm, fw), lo_map),
            pl.BlockSpec((tm, fw), hi_map),
            pl.BlockSpec((tm, fw), lambda i: (i, SLOT_SF * COL_BLOCK // fw)),
            pl.BlockSpec((None, fw, fw), lambda i: (layer, 0, 0)),
        ],
        out_specs=pl.BlockSpec((tm, fw), lambda i: (i, 0)),
        out_shape=jax.ShapeDtypeStruct((n, fw), BF16),
        compiler_params=_params("parallel"),
        name="fourier_mix",
    )(f_lo, f_hi, z16, w_fmix)


def _merge_kernel(yf_ref, yr_ref, wf_ref, wr_ref, gf_ref, gr_ref, o_ref):
    pf = jnp.dot(yf_ref[...], wf_ref[...], preferred_element_type=F32)
    pr = jnp.dot(yr_ref[...], wr_ref[...], preferred_element_type=F32)
    o_ref[...] = (gf_ref[...].astype(F32) * pf + gr_ref[...].astype(F32) * pr).astype(BF16)


def _merge(yf, yr, z16, w_bf, w_br, layer, tm):
    n = yr.shape[0]
    tn = COL_BLOCK
    return pl.pallas_call(
        _merge_kernel,
        grid=(n // tm, D_MODEL // tn),
        in_specs=[
            pl.BlockSpec((tm, FOURIER_WIDTH), lambda i, j: (i, 0)),
            pl.BlockSpec((tm, LRU_WIDTH), lambda i, j: (i, 0)),
            pl.BlockSpec((None, FOURIER_WIDTH, tn), lambda i, j: (layer, 0, j)),
            pl.BlockSpec((None, LRU_WIDTH, tn), lambda i, j: (layer, 0, j)),
            pl.BlockSpec((tm, tn), lambda i, j: (i, SLOT_SGF + j)),
            pl.BlockSpec((tm, tn), lambda i, j: (i, SLOT_SGR + j)),
        ],
        out_specs=pl.BlockSpec((tm, tn), lambda i, j: (i, j)),
        out_shape=jax.ShapeDtypeStruct((n, D_MODEL), BF16),
        compiler_params=_params("parallel", "parallel"),
        name="branch_merge",
    )(yf, yr, w_bf, w_br, z16, z16)


_OUT_COLS = D_MODEL // COL_BLOCK


def _row_norm(xrow, ng_ref):
    ssq = None
    for c in range(_OUT_COLS):
        v = xrow[c]
        part = jnp.sum(v * v, axis=-1, keepdims=True)
        ssq = part if ssq is None else ssq + part
    inv = lax.rsqrt(ssq * (1.0 / D_MODEL) + EPS)
    for c in range(_OUT_COLS):
        cols = slice(c * COL_BLOCK, (c + 1) * COL_BLOCK)
        yield cols, (xrow[c] * inv) * ng_ref[:, cols]


def _outproj_mid_kernel(m_ref, w_ref, x_ref, g_ref, ng_ref, sc_ref, sh_ref, xo_ref, h_ref, xrow):
    j = pl.program_id(1)
    xn = x_ref[...] + g_ref[...] * jnp.dot(m_ref[...], w_ref[...], preferred_element_type=F32)
    xo_ref[...] = xn
    xrow[j] = xn

    @pl.when(j == _OUT_COLS - 1)
    def _():
        for cols, y in _row_norm(xrow, ng_ref):
            h_ref[:, cols] = (y * (1.0 + sc_ref[:, cols]) + sh_ref[:, cols]).astype(BF16)


def _outproj_last_kernel(m_ref, w_ref, x_ref, g_ref, ng_ref, y_ref, xrow):
    j = pl.program_id(1)
    xrow[j] = x_ref[...] + g_ref[...] * jnp.dot(m_ref[...], w_ref[...], preferred_element_type=F32)

    @pl.when(j == _OUT_COLS - 1)
    def _():
        for cols, y in _row_norm(xrow, ng_ref):
            y_ref[:, cols] = y


def _outproj(merged, w_out, layer, x, gate, row_of, tm, norm_g, scale=None, shift=None):
    n = x.shape[0]
    tn = COL_BLOCK
    last = scale is None
    row_block = pl.BlockSpec((tm, D_MODEL), lambda i, j: (i, 0))
    col_block = pl.BlockSpec((tm, tn), lambda i, j: (i, j))
    mod_row = pl.BlockSpec((None, 1, D_MODEL), lambda i, j: (row_of(i), 0, 0))
    in_specs = [
        row_block,
        pl.BlockSpec((None, D_MODEL, tn), lambda i, j: (layer, 0, j)),
        col_block,
        pl.BlockSpec((None, 1, tn), lambda i, j: (row_of(i), 0, j)),
        pl.BlockSpec((1, D_MODEL), lambda i, j: (0, 0)),
    ]
    scratch = [pltpu.VMEM((_OUT_COLS, tm, tn), F32)]
    if last:
        return pl.pallas_call(
            _outproj_last_kernel,
            grid=(n // tm, _OUT_COLS),
            in_specs=in_specs,
            out_specs=row_block,
            out_shape=jax.ShapeDtypeStruct((n, D_MODEL), F32),
            scratch_shapes=scratch,
            compiler_params=_params("parallel", "arbitrary"),
            name="out_proj_final_norm",
        )(merged, w_out, x, gate, norm_g)
    return pl.pallas_call(
        _outproj_mid_kernel,
        grid=(n // tm, _OUT_COLS),
        in_specs=in_specs + [mod_row, mod_row],
        out_specs=[col_block, row_block],
        out_shape=[jax.ShapeDtypeStruct((n, D_MODEL), F32),
                   jax.ShapeDtypeStruct((n, D_MODEL), BF16)],
        scratch_shapes=scratch,
        compiler_params=_params("parallel", "arbitrary"),
        name="out_proj_norm",
    )(merged, w_out, x, gate, norm_g, scale, shift)


class _Tiles(NamedTuple):
    norm: int
    inproj: int
    merge: int
    outproj: int
    lru_channels: Optional[int]
    dft_rows: Optional[int]
    dft_batch: Optional[int]


_LONG_SEQ = 1024


def _tiles(seq):
    if seq >= _LONG_SEQ:
        return _Tiles(norm=512, inproj=1024, merge=1024, outproj=512, lru_channels=None,
                      dft_rows=512, dft_batch=None)
    return _Tiles(norm=512, inproj=1024, merge=1024, outproj=512, lru_channels=LRU_WIDTH,
                  dft_rows=None, dft_batch=8)


def _channel_dft_table():
    k = jnp.arange(FOURIER_GROUP_CH, dtype=jnp.int32)
    ang = ((k[:, None] * k[None, :]) % FOURIER_GROUP_CH).astype(F32) * (2.0 * math.pi / FOURIER_GROUP_CH)
    scale = 1.0 / math.sqrt(FOURIER_GROUP_CH)
    return jnp.concatenate([jnp.cos(ang) * scale, jnp.sin(ang) * scale], axis=1).astype(BF16)


def _per_head(v, parts):
    return (v.reshape(parts, LRU_HEADS, LRU_BLOCK).transpose(1, 0, 2)
            .reshape(LRU_HEADS, 1, parts * LRU_BLOCK))


def _stream(x, batch, seq, row_of_token_tile, h0_layers, mods, big, small, cs, final_g):
    n = batch * seq
    x = x.reshape(n, D_MODEL)
    tiles = _tiles(seq)
    long_seq = tiles.dft_rows is not None
    if long_seq:
        dft_tables = _tdft_sym_tables(seq, tiles.dft_rows)
        half_tiles = seq // 2 // tiles.merge
    else:
        dft_tables = _tdft_short_tables(seq)
        half_tiles = None
    states = []
    shift, scale, _ = mods[0]
    h = _norm_mod(x, small[0]["norm_g"], scale, shift, row_of_token_tile(tiles.norm), tiles.norm)
    for l in range(DEPTH):
        w = small[l]
        gate = mods[l][2]
        z16, a_s, rin = _inproj(h, big["w_in"], l, cs, tiles.inproj)
        if long_seq:
            f_lo, f_hi, yr = _tdft_lru(
                z16, a_s, rin, dft_tables, w["conv_w"], w["conv_b"], w["wg"], w["bg"], w["cq"],
                h0_layers[l], batch, seq, tiles.dft_rows)
        else:
            f_lo = f_hi = _tdft_short(z16, a_s, dft_tables, batch, seq, tiles.dft_batch)
            yr, st = _lru(rin, z16, w["conv_w"], w["conv_b"], w["wg"], w["bg"], w["cq"],
                          h0_layers[l], batch, seq, tiles.lru_channels)
            states.append(st)
        yf = _fmix(f_lo, f_hi, z16, big["w_fmix"], l, n, tiles.merge, half_tiles)
        merged = _merge(yf, yr, z16, big["w_bf"], big["w_br"], l, tiles.merge)
        row_of = row_of_token_tile(tiles.outproj)
        if l + 1 < DEPTH:
            next_shift, next_scale, _ = mods[l + 1]
            x, h = _outproj(merged, big["w_out"], l, x, gate, row_of, tiles.outproj,
                            small[l + 1]["norm_g"], next_scale, next_shift)
        else:
            y = _outproj(merged, big["w_out"], l, x, gate, row_of, tiles.outproj, final_g)
    return y.reshape(batch, seq, D_MODEL), states


def kernel(x_prompt, x_sample, state_lru, c, c_ctx, norm_g, w_ada, b_ada, w_in, w_fmix,
           conv_w, conv_b, w_gate_a, b_gate_a, w_gate_x, b_gate_x, lam,
           w_branch_f, w_branch_r, w_out, final_g):
    batch, seq, _ = x_prompt.shape
    dec_batch, dec_seq, _ = x_sample.shape
    assert 1 + dec_batch <= COND_ROWS

    cond = jnp.zeros((COND_ROWS, D_MODEL), F32)
    cond = cond.at[0].set(c_ctx).at[1:1 + dec_batch].set(c)
    mod = _ada(cond, w_ada, b_ada)
    mods = []
    for l in range(DEPTH):
        parts = jnp.split(mod[l], 3, axis=-1)
        mods.append(tuple(p.reshape(COND_ROWS, 1, D_MODEL) for p in parts))

    big = dict(w_in=w_in.astype(BF16), w_fmix=w_fmix.astype(BF16), w_bf=w_branch_f.astype(BF16),
               w_br=w_branch_r.astype(BF16), w_out=w_out.astype(BF16))
    small = []
    for l in range(DEPTH):
        wg = (0.5 * jnp.concatenate([w_gate_a[l, 0], w_gate_x[l, 0], w_gate_a[l, 1], w_gate_x[l, 1]],
                                    axis=-1)).astype(BF16)
        bg = _per_head(0.5 * jnp.stack([b_gate_a[l, 0], b_gate_x[l, 0], b_gate_a[l, 1], b_gate_x[l, 1]]), 4)
        cq = _per_head((0.25 * LRU_C) * jax.nn.softplus(-lam[l]), 2)
        small.append(dict(norm_g=norm_g[l].reshape(1, D_MODEL), conv_w=conv_w[l],
                          conv_b=conv_b[l].reshape(1, LRU_WIDTH), wg=wg, bg=bg, cq=cq))

    cs = _channel_dft_table()
    final_g = final_g.reshape(1, D_MODEL)

    zeros_h0 = jnp.zeros((batch, 2, LRU_WIDTH), F32)
    y_prompt, ctx_states = _stream(
        x_prompt, batch, seq, lambda tm: (lambda i: 0), [zeros_h0] * DEPTH,
        mods, big, small, cs, final_g)
    new_state = jnp.stack(ctx_states, axis=1)

    y_sample, _ = _stream(
        x_sample, dec_batch, dec_seq, lambda tm: (lambda i: 1 + (i * tm) // dec_seq),
        [state_lru[:, l] for l in range(DEPTH)], mods, big, small, cs, final_g)
    return (y_prompt, y_sample, new_state)
```

```python
import functools
import math
from typing import NamedTuple, Optional

import jax
import jax.numpy as jnp
from jax import lax
from jax.experimental import pallas as pl
from jax.experimental.pallas import tpu as pltpu

F32 = jnp.float32
BF16 = jnp.bfloat16

D_MODEL = 4096
DEPTH = 2
FOURIER_WIDTH = D_MODEL // 2
FOURIER_GROUP_CH = 256
LRU_WIDTH = D_MODEL // 2
LRU_BLOCK = 128
LRU_HEADS = LRU_WIDTH // LRU_BLOCK
CONV_WIDTH = 4
LRU_C = 8.0
IN_WIDTH = 2 * FOURIER_WIDTH + 2 * LRU_WIDTH + 2 * D_MODEL
EPS = 1e-6

SUBLANES = 8
LANES = 128
VMEM_LIMIT_BYTES = 58 * 1024 * 1024
F32_MIN_NORMAL = 2.0 ** -126

COND_ROWS = 16
COL_BLOCK = 1024
SLOT_AC, SLOT_SF, SLOT_SR, SLOT_SGF, SLOT_SGR = 0, 2, 4, 6, 10
PACKED_BLOCKS = 14


def _sigmoid(x):
    return 0.5 * jnp.tanh(0.5 * x) + 0.5


def _params(*sem):
    return pltpu.CompilerParams(dimension_semantics=sem, vmem_limit_bytes=VMEM_LIMIT_BYTES)


def _ada_kernel(c_ref, w_ref, b_ref, o_ref):
    c = c_ref[...]
    s = (c * _sigmoid(c)).astype(BF16)
    w = w_ref[...].astype(BF16)
    o_ref[...] = jnp.dot(s, w, preferred_element_type=F32) + b_ref[...]


def _ada(cond, w_ada, b_ada):
    tn = 512
    n_out = 3 * D_MODEL
    return pl.pallas_call(
        _ada_kernel,
        grid=(DEPTH, n_out // tn),
        in_specs=[
            pl.BlockSpec((COND_ROWS, D_MODEL), lambda l, j: (0, 0)),
            pl.BlockSpec((None, D_MODEL, tn), lambda l, j: (l, 0, j)),
            pl.BlockSpec((None, 1, tn), lambda l, j: (l, 0, j)),
        ],
        out_specs=pl.BlockSpec((None, COND_ROWS, tn), lambda l, j: (l, 0, j)),
        out_shape=jax.ShapeDtypeStruct((DEPTH, COND_ROWS, n_out), F32),
        compiler_params=_params("parallel", "parallel"),
        name="ada_mod",
    )(cond, w_ada, b_ada.reshape(DEPTH, 1, n_out))


def _norm_mod_kernel(x_ref, g_ref, sc_ref, sh_ref, o_ref):
    x = x_ref[...]
    ms = jnp.mean(x * x, axis=-1, keepdims=True)
    y = (x * lax.rsqrt(ms + EPS)) * g_ref[...]
    o_ref[...] = (y * (1.0 + sc_ref[...]) + sh_ref[...]).astype(o_ref.dtype)


def _norm_mod(x, g, scale, shift, row_of, tm):
    n = x.shape[0]
    return pl.pallas_call(
        _norm_mod_kernel,
        grid=(n // tm,),
        in_specs=[
            pl.BlockSpec((tm, D_MODEL), lambda i: (i, 0)),
            pl.BlockSpec((1, D_MODEL), lambda i: (0, 0)),
            pl.BlockSpec((None, 1, D_MODEL), lambda i: (row_of(i), 0, 0)),
            pl.BlockSpec((None, 1, D_MODEL), lambda i: (row_of(i), 0, 0)),
        ],
        out_specs=pl.BlockSpec((tm, D_MODEL), lambda i: (i, 0)),
        out_shape=jax.ShapeDtypeStruct((n, D_MODEL), BF16),
        compiler_params=_params("parallel"),
        name="norm_mod",
    )(x, g, scale, shift)


def _inproj_kernel(h_ref, w_ref, cs_ref, z16_ref, as_ref, rin_ref):
    j = pl.program_id(1)

    def z():
        return jnp.dot(h_ref[...], w_ref[...], preferred_element_type=F32)

    @pl.when(j < 2)
    def _():
        zb = z().astype(BF16)
        for g in range(COL_BLOCK // FOURIER_GROUP_CH):
            cols = slice(g * FOURIER_GROUP_CH, (g + 1) * FOURIER_GROUP_CH)
            t = jnp.dot(zb[:, cols], cs_ref[...], preferred_element_type=F32)
            z16_ref[:, cols] = t[:, :FOURIER_GROUP_CH].astype(BF16)
            as_ref[:, cols] = t[:, FOURIER_GROUP_CH:].astype(BF16)

    @pl.when(((j >= 2) & (j < 4)) | ((j >= 6) & (j < 8)))
    def _():
        v = z()
        z16_ref[...] = (v * _sigmoid(v)).astype(BF16)

    @pl.when((j >= 4) & (j < 6))
    def _():
        rin_ref[...] = z()

    @pl.when(j >= 8)
    def _():
        z16_ref[...] = _sigmoid(z()).astype(BF16)


def _packed_slot(j):
    return jnp.where(j < 4, j, jnp.where(j < 6, 3, j - 2))


def _inproj(h, w_in, layer, cs, tm):
    n = h.shape[0]
    tn = COL_BLOCK
    return pl.pallas_call(
        _inproj_kernel,
        grid=(n // tm, IN_WIDTH // tn),
        in_specs=[
            pl.BlockSpec((tm, D_MODEL), lambda i, j: (i, 0)),
            pl.BlockSpec((None, D_MODEL, tn), lambda i, j: (layer, 0, j)),
            pl.BlockSpec((FOURIER_GROUP_CH, 2 * FOURIER_GROUP_CH), lambda i, j: (0, 0)),
        ],
        out_specs=[
            pl.BlockSpec((tm, tn), lambda i, j: (i, _packed_slot(j))),
            pl.BlockSpec((tm, tn), lambda i, j: (i, jnp.minimum(j, 1))),
            pl.BlockSpec((tm, tn), lambda i, j: (i, jnp.clip(j - 4, 0, 1))),
        ],
        out_shape=[
            jax.ShapeDtypeStruct((n, PACKED_BLOCKS * COL_BLOCK), BF16),
            jax.ShapeDtypeStruct((n, FOURIER_WIDTH), BF16),
            jax.ShapeDtypeStruct((n, LRU_WIDTH), F32),
        ],
        compiler_params=_params("parallel", "arbitrary"),
        name="in_proj",
    )(h, w_in, cs)


_HALO = SUBLANES
_GATE_ROWS = 256
_SEGMENTS = SUBLANES
_SEG_PAD = SUBLANES


def _conv_taps(xe, cw_ref, cb_ref, rows):
    ext = rows + 2 * _HALO
    inner = slice(_HALO, _HALO + rows)
    xc = cb_ref[...] + pltpu.roll(xe, 1, 0)[inner] * cw_ref[0:1, :]
    xc = xc + xe[inner] * cw_ref[1:2, :]
    xc = xc + pltpu.roll(xe, ext - 1, 0)[inner] * cw_ref[2:3, :]
    return xc + pltpu.roll(xe, ext - 2, 0)[inner] * cw_ref[3:4, :]


def _gate_terms(gh, xh, cq):
    out = []
    for d in range(2):
        base = 2 * d * LRU_BLOCK
        t_a = jnp.tanh(gh[:, base:base + LRU_BLOCK])
        t_x = jnp.tanh(gh[:, base + LRU_BLOCK:base + 2 * LRU_BLOCK])
        c = cq[:, d * LRU_BLOCK:(d + 1) * LRU_BLOCK]
        s = jnp.tanh(c * t_a + c)
        inv = 1.0 / (1.0 + s)
        root = s * lax.rsqrt(jnp.maximum(s, F32_MIN_NORMAL))
        a = (1.0 - s) * inv
        u = (inv * root) * (xh * t_x + xh)
        out.append((a, u))
    return out


def _block_scan(a, b, row, reverse):
    for s in (1, 2, 4):
        if reverse:
            m, sh = row < SUBLANES - s, SUBLANES - s
        else:
            m, sh = row >= s, s
        a_sh = jnp.where(m, pltpu.roll(a, sh, 0), 1.0)
        b_sh = jnp.where(m, pltpu.roll(b, sh, 0), 0.0)
        b = b + a * b_sh
        a = a * a_sh
    return a, b


_SCAN_ROWS = 2 * SUBLANES


def _scan_both(a_f, u_f, a_b, u_b, h_f, h_b, h0_ref, st_ref, *, seq, tc, cs):
    row = lax.broadcasted_iota(jnp.int32, (SUBLANES, cs), 0)
    for c0 in range(0, tc, cs):
        cols = slice(c0, c0 + cs)

        def body(k, carry, cols=cols):
            carry_f, carry_b = carry
            rf = pl.multiple_of(k * _SCAN_ROWS, _SCAN_ROWS)
            rb = pl.multiple_of(seq - _SCAN_ROWS - k * _SCAN_ROWS, _SCAN_ROWS)
            a2 = a_f[pl.ds(rf, _SCAN_ROWS), cols]
            b2 = u_f[pl.ds(rf, _SCAN_ROWS), cols].astype(F32)
            hs = []
            for p in (0, 1):
                blk = slice(p * SUBLANES, (p + 1) * SUBLANES)
                a, b = _block_scan(a2[blk], b2[blk], row, False)
                h = a * carry_f + b
                hs.append(h)
                carry_f = jnp.broadcast_to(h[SUBLANES - 1:SUBLANES, :], (SUBLANES, cs))
            h_f[pl.ds(rf, _SCAN_ROWS), cols] = jnp.concatenate(hs, axis=0)

            a2 = a_b[pl.ds(rb, _SCAN_ROWS), cols]
            b2 = u_b[pl.ds(rb, _SCAN_ROWS), cols].astype(F32)
            hs = [None, None]
            for p in (1, 0):
                blk = slice(p * SUBLANES, (p + 1) * SUBLANES)
                a, b = _block_scan(a2[blk], b2[blk], row, True)
                h = a * carry_b + b
                hs[p] = h
                carry_b = jnp.broadcast_to(h[0:1, :], (SUBLANES, cs))
            h_b[pl.ds(rb, _SCAN_ROWS), cols] = jnp.concatenate(hs, axis=0)
            return carry_f, carry_b

        init = (jnp.broadcast_to(h0_ref[0:1, cols], (SUBLANES, cs)),
                jnp.broadcast_to(h0_ref[1:2, cols], (SUBLANES, cs)))
        fin_f, fin_b = lax.fori_loop(0, seq // _SCAN_ROWS, body, init)
        st_ref[0:1, cols] = fin_f[0:1, :]
        st_ref[1:2, cols] = fin_b[0:1, :]


def _write_gated_sum(h_f, h_b, sr_ref, y_ref, seq):
    def body(c, carry):
        r = pl.ds(pl.multiple_of(c * _GATE_ROWS, _GATE_ROWS), _GATE_ROWS)
        y_ref[r, :] = ((h_f[r, :] + h_b[r, :]) * sr_ref[r, :].astype(F32)).astype(BF16)
        return carry
    lax.fori_loop(0, seq // _GATE_ROWS, body, 0)


def _lru_kernel(rin_ref, sr_ref, cw_ref, cb_ref, wg_ref, bg_ref, cq_ref, h0_ref,
                y_ref, st_ref, xpad, a_f, u_f, a_b, u_b, *, seq, tc, cs):
    heads = tc // LRU_BLOCK
    n_chunks = seq // _GATE_ROWS

    def rows(c):
        return pl.multiple_of(c * _GATE_ROWS, _GATE_ROWS)

    zero_halo = jnp.zeros((_HALO, tc), F32)
    xpad[0:_HALO, :] = zero_halo
    xpad[seq + _HALO:seq + 2 * _HALO, :] = zero_halo

    def copy_body(c, carry):
        r0 = rows(c)
        xpad[pl.ds(r0 + _HALO, _GATE_ROWS), :] = rin_ref[pl.ds(r0, _GATE_ROWS), :]
        return carry
    lax.fori_loop(0, n_chunks, copy_body, 0)

    def gate_body(c, carry):
        r0 = rows(c)
        xc = _conv_taps(xpad[pl.ds(r0, _GATE_ROWS + 2 * _HALO), :], cw_ref, cb_ref, _GATE_ROWS)
        for hh in range(heads):
            cols = slice(hh * LRU_BLOCK, (hh + 1) * LRU_BLOCK)
            xh = xc[:, cols]
            gh = jnp.dot(xh.astype(BF16), wg_ref[hh], preferred_element_type=F32) + bg_ref[hh]
            (af, uf), (ab, ub) = _gate_terms(gh, xh, cq_ref[hh])
            a_f[pl.ds(r0, _GATE_ROWS), cols] = af
            u_f[pl.ds(r0, _GATE_ROWS), cols] = uf
            a_b[pl.ds(r0, _GATE_ROWS), cols] = ab
            u_b[pl.ds(r0, _GATE_ROWS), cols] = ub
        return carry
    lax.fori_loop(0, n_chunks, gate_body, 0)

    _scan_both(a_f, u_f, a_b, u_b, a_f, a_b, h0_ref, st_ref, seq=seq, tc=tc, cs=cs)
    _write_gated_sum(a_f, a_b, sr_ref, y_ref, seq)


def _lru(rin, z16, conv_w, conv_b, wg, bg, cq, h0, batch, seq, tc):
    n = batch * seq
    heads = tc // LRU_BLOCK
    cs = min(tc, 512)
    sr_block0 = SLOT_SR * COL_BLOCK // tc
    kern = functools.partial(_lru_kernel, seq=seq, tc=tc, cs=cs)
    return pl.pallas_call(
        kern,
        grid=(batch, LRU_WIDTH // tc),
        in_specs=[
            pl.BlockSpec((seq, tc), lambda b, c: (b, c)),
            pl.BlockSpec((seq, tc), lambda b, c: (b, sr_block0 + c)),
            pl.BlockSpec((CONV_WIDTH, tc), lambda b, c: (0, c)),
            pl.BlockSpec((1, tc), lambda b, c: (0, c)),
            pl.BlockSpec((heads, LRU_BLOCK, 4 * LRU_BLOCK), lambda b, c: (c, 0, 0)),
            pl.BlockSpec((heads, 1, 4 * LRU_BLOCK), lambda b, c: (c, 0, 0)),
            pl.BlockSpec((heads, 1, 2 * LRU_BLOCK), lambda b, c: (c, 0, 0)),
            pl.BlockSpec((None, 2, tc), lambda b, c: (b, 0, c)),
        ],
        out_specs=[
            pl.BlockSpec((seq, tc), lambda b, c: (b, c)),
            pl.BlockSpec((None, 2, tc), lambda b, c: (b, 0, c)),
        ],
        out_shape=[
            jax.ShapeDtypeStruct((n, LRU_WIDTH), BF16),
            jax.ShapeDtypeStruct((batch, 2, LRU_WIDTH), F32),
        ],
        scratch_shapes=[
            pltpu.VMEM((seq + 2 * _HALO, tc), F32),
            pltpu.VMEM((seq, tc), F32),
            pltpu.VMEM((seq, tc), F32),
            pltpu.VMEM((seq, tc), F32),
            pltpu.VMEM((seq, tc), F32),
        ],
        compiler_params=_params("parallel", "parallel"),
        name="conv_rglru",
    )(rin, z16, conv_w, conv_b, wg, bg, cq, h0)


def _segment_permutation():
    per_chunk = _GATE_ROWS // _SEGMENTS
    r = jnp.arange(_GATE_ROWS, dtype=jnp.int32)
    src = (r % _SEGMENTS) * per_chunk + r // _SEGMENTS
    return (jnp.arange(_GATE_ROWS, dtype=jnp.int32)[None, :] == src[:, None]).astype(BF16)


_MIRROR_PAD = 16


def _tdft_lru_kernel(c_ref, s_ref, ac_ref, as_ref, sel_ref, rin_ref, cw_ref, cb_ref, wg_ref,
                     bg_ref, cq_ref, sr_ref, h0_ref, pm_ref, pmt_ref, lo_ref, hi_ref, y_ref,
                     xs, a_f, u_f, a_b, u_b, *, tm, seq):
    seg = seq // _SEGMENTS
    pitch = seg + _SEG_PAD
    for s in range(_SEGMENTS):
        xs[s * pitch:s * pitch + seg, :] = rin_ref[s * seg:(s + 1) * seg, :]
    row = lax.broadcasted_iota(jnp.int32, (_SEGMENTS, LRU_BLOCK), 0)

    def positions(j):
        if 0 <= j < seg:
            return xs[pl.ds(j, _SEGMENTS, stride=pitch), :]
        if j < 0:
            v = xs[pl.ds(seg + j, _SEGMENTS, stride=pitch), :]
            return jnp.where(row >= 1, pltpu.roll(v, 1, 0), 0.0)
        v = xs[pl.ds(j - seg, _SEGMENTS, stride=pitch), :]
        return jnp.where(row < _SEGMENTS - 1, pltpu.roll(v, _SEGMENTS - 1, 0), 0.0)

    taps = [jnp.broadcast_to(cw_ref[i:i + 1, :], (_SEGMENTS, LRU_BLOCK)) for i in range(CONV_WIDTH)]
    bias = jnp.broadcast_to(cb_ref[...], (_SEGMENTS, LRU_BLOCK))
    per_chunk = _GATE_ROWS // _SEGMENTS

    n_chunks = seq // _GATE_ROWS
    blocks = [slice(jj * _SEGMENTS, (jj + 1) * _SEGMENTS) for jj in range(per_chunk)]

    def chunk_rows(k):
        return slice(k * _GATE_ROWS, (k + 1) * _GATE_ROWS)

    def chunk_slabs(k):
        return [slice(s * seg + k * per_chunk, s * seg + (k + 1) * per_chunk)
                for s in range(_SEGMENTS)]

    def dft_slice(k, acc):
        r = chunk_rows(k)
        dp = jnp.dot(c_ref[:, r], ac_ref[r, :], preferred_element_type=F32)
        dq = jnp.dot(s_ref[:, r], as_ref[r, :], preferred_element_type=F32)
        return (dp, dq) if acc is None else (acc[0] + dp, acc[1] + dq)

    zeros = jnp.zeros((_SEGMENTS, LRU_BLOCK), F32)
    ones = jnp.ones((_SEGMENTS, LRU_BLOCK), F32)
    end_f, prod_f, end_b, prod_b = zeros, ones, zeros, ones
    acc = None
    for k in range(n_chunks):
        r = chunk_rows(k)
        j0 = k * per_chunk
        vs = [positions(j) for j in range(j0 - 1, j0 + per_chunk + CONV_WIDTH - 2)]
        xc = jnp.concatenate(
            [((bias + vs[jj] * taps[0]) + vs[jj + 1] * taps[1] + vs[jj + 2] * taps[2])
             + vs[jj + 3] * taps[3] for jj in range(per_chunk)], axis=0)
        gh = jnp.dot(xc.astype(BF16), wg_ref[...], preferred_element_type=F32) + bg_ref[...]
        acc = dft_slice(k, acc)

        (af, uf), (ab, ub) = _gate_terms(gh, xc, cq_ref[...])
        a_f[r, :] = af
        u_f[r, :] = uf
        a_b[r, :] = ab
        u_b[r, :] = ub
        for blk in blocks:
            end_f = af[blk] * end_f + uf[blk]
            prod_f = prod_f * af[blk]
            end_b = end_b + prod_b * ub[blk]
            prod_b = prod_b * ab[blk]

    p, q = acc
    lo_ref[...] = (p[:tm] + q[:tm]).astype(BF16)
    mirrored = (p - q).astype(BF16)
    hi_ref[...] = jnp.dot(sel_ref[...], mirrored, preferred_element_type=F32).astype(BF16)

    h0_f = jnp.broadcast_to(h0_ref[0:1, :], (_SEGMENTS, LRU_BLOCK))
    h0_b = jnp.broadcast_to(h0_ref[1:2, :], (_SEGMENTS, LRU_BLOCK))
    a_cum, b_cum = _block_scan(prod_f, end_f, row, False)
    true_end_f = a_cum * h0_f + b_cum
    h_fwd = jnp.where(row >= 1, pltpu.roll(true_end_f, 1, 0), h0_f)
    a_cum, b_cum = _block_scan(prod_b, end_b, row, True)
    true_end_b = a_cum * h0_b + b_cum
    h_bwd = jnp.where(row < _SEGMENTS - 1, pltpu.roll(true_end_b, _SEGMENTS - 1, 0), h0_b)

    def park_sr(k):
        sr_pos = jnp.concatenate([sr_ref[sl, :] for sl in chunk_slabs(k)], axis=0)
        xs[chunk_rows(k), :] = jnp.dot(pm_ref[...], sr_pos, preferred_element_type=F32)

    def emit_y(k):
        r = chunk_rows(k)
        y_seg = ((u_f[r, :] + u_b[r, :]) * xs[r, :]).astype(BF16)
        y_pos = jnp.dot(pmt_ref[...], y_seg, preferred_element_type=F32).astype(BF16)
        for s, sl in enumerate(chunk_slabs(k)):
            y_ref[sl, :] = y_pos[s * per_chunk:(s + 1) * per_chunk, :]

    half_chunks = n_chunks // 2
    for k in range(n_chunks):
        if k < half_chunks:
            park_sr(k)
            park_sr(n_chunks - 1 - k)
        rf, rb = chunk_rows(k), chunk_rows(n_chunks - 1 - k)
        af_c, uf_c, ab_c, ub_c = a_f[rf, :], u_f[rf, :], a_b[rb, :], u_b[rb, :]
        hfs, hbs = [], [None] * per_chunk
        for jj in range(per_chunk):
            h_fwd = af_c[blocks[jj]] * h_fwd + uf_c[blocks[jj]]
            hfs.append(h_fwd)
            jb = per_chunk - 1 - jj
            h_bwd = ab_c[blocks[jb]] * h_bwd + ub_c[blocks[jb]]
            hbs[jb] = h_bwd
        u_f[rf, :] = jnp.concatenate(hfs, axis=0)
        u_b[rb, :] = jnp.concatenate(hbs, axis=0)
        if k >= half_chunks:
            emit_y(k)
            emit_y(n_chunks - 1 - k)


def _tdft_sym_tables(seq, tm):
    half_tiles = seq // 2 // tm
    ext = tm + _MIRROR_PAD
    k = (jnp.arange(half_tiles, dtype=jnp.int32)[:, None] * tm
         + jnp.arange(ext, dtype=jnp.int32)[None, :]).reshape(-1)
    side = math.isqrt(seq)
    assert side * side == seq
    tt = jnp.arange(side, dtype=jnp.int32)
    coarse = ((k[:, None] * tt[None, :]) % side).astype(F32) * (2.0 * math.pi / side)
    fine = ((k[:, None] * tt[None, :]) % seq).astype(F32) * (2.0 * math.pi / seq)
    t = jnp.arange(seq, dtype=jnp.int32)
    pick_coarse = (tt[:, None] == (t // side)[None, :]).astype(F32)
    pick_fine = (tt[:, None] == (t % side)[None, :]).astype(F32)
    expand = functools.partial(jnp.dot, precision=lax.Precision.HIGHEST)
    ca, sa = expand(jnp.cos(coarse), pick_coarse), expand(jnp.sin(coarse), pick_coarse)
    cb, sb = expand(jnp.cos(fine), pick_fine), expand(jnp.sin(fine), pick_fine)
    scale = 1.0 / math.sqrt(seq)
    cmat = ((ca * cb - sa * sb) * scale).astype(BF16).reshape(half_tiles, ext, seq)
    smat_neg = ((sa * cb + ca * sb) * (-scale)).astype(BF16).reshape(half_tiles, ext, seq)
    r = jnp.arange(tm, dtype=jnp.int32)
    sel = (jnp.arange(ext, dtype=jnp.int32)[None, :] == (tm - r)[:, None]).astype(BF16)
    return cmat, smat_neg, sel


def _tdft_lru(z16, a_s, rin, tables, conv_w, conv_b, wg, bg, cq, h0, batch, seq, tm):
    cmat, smat_neg, sel = tables
    tn = COL_BLOCK // 2
    ac_block0 = SLOT_AC * COL_BLOCK // tn
    sr_block0 = SLOT_SR * COL_BLOCK // LRU_BLOCK
    perm = _segment_permutation()
    half_tiles = seq // 2 // tm
    col_tiles = FOURIER_WIDTH // tn
    assert col_tiles * half_tiles == LRU_HEADS
    ext = tm + _MIRROR_PAD
    n = batch * seq
    half = jax.ShapeDtypeStruct((n // 2, FOURIER_WIDTH), BF16)

    def head(c, i):
        return c * half_tiles + i

    head_cols = pl.BlockSpec((seq, LRU_BLOCK), lambda i, b, c: (b, head(c, i)))
    perm_spec = pl.BlockSpec((_GATE_ROWS, _GATE_ROWS), lambda i, b, c: (0, 0))
    head_f32 = pltpu.VMEM((seq, LRU_BLOCK), F32)
    return pl.pallas_call(
        functools.partial(_tdft_lru_kernel, tm=tm, seq=seq),
        grid=(half_tiles, batch, col_tiles),
        in_specs=[
            pl.BlockSpec((None, ext, seq), lambda i, b, c: (i, 0, 0)),
            pl.BlockSpec((None, ext, seq), lambda i, b, c: (i, 0, 0)),
            pl.BlockSpec((seq, tn), lambda i, b, c: (b, ac_block0 + c)),
            pl.BlockSpec((seq, tn), lambda i, b, c: (b, c)),
            pl.BlockSpec((tm, ext), lambda i, b, c: (0, 0)),
            head_cols,
            pl.BlockSpec((CONV_WIDTH, LRU_BLOCK), lambda i, b, c: (0, head(c, i))),
            pl.BlockSpec((1, LRU_BLOCK), lambda i, b, c: (0, head(c, i))),
            pl.BlockSpec((None, LRU_BLOCK, 4 * LRU_BLOCK), lambda i, b, c: (head(c, i), 0, 0)),
            pl.BlockSpec((None, 1, 4 * LRU_BLOCK), lambda i, b, c: (head(c, i), 0, 0)),
            pl.BlockSpec((None, 1, 2 * LRU_BLOCK), lambda i, b, c: (head(c, i), 0, 0)),
            pl.BlockSpec((seq, LRU_BLOCK), lambda i, b, c: (b, sr_block0 + head(c, i))),
            pl.BlockSpec((None, 2, LRU_BLOCK), lambda i, b, c: (b, 0, head(c, i))),
            perm_spec, perm_spec,
        ],
        out_specs=[
            pl.BlockSpec((tm, tn), lambda i, b, c: (b * half_tiles + i, c)),
            pl.BlockSpec((tm, tn), lambda i, b, c: (b * half_tiles + half_tiles - 1 - i, c)),
            head_cols,
        ],
        out_shape=[half, half, jax.ShapeDtypeStruct((n, LRU_WIDTH), BF16)],
        scratch_shapes=[
            pltpu.VMEM((_SEGMENTS * (seq // _SEGMENTS + _SEG_PAD), LRU_BLOCK), F32),
            head_f32, head_f32, head_f32, head_f32,
        ],
        compiler_params=_params("parallel", "parallel", "parallel"),
        name="pos_dft_rglru",
    )(cmat, smat_neg, z16, a_s, sel, rin, conv_w, conv_b, wg, bg, cq, z16, h0, perm, perm.T)


def _tdft_multi_kernel(c_ref, s_ref, ac_ref, as_ref, o_ref, *, nb, seq):
    for b in range(nb):
        r = slice(b * seq, (b + 1) * seq)
        o_ref[r, :] = (jnp.dot(c_ref[...], ac_ref[r, :], preferred_element_type=F32)
                       + jnp.dot(s_ref[...], as_ref[r, :], preferred_element_type=F32)).astype(BF16)


def _tdft_short_tables(seq):
    k = jnp.arange(seq, dtype=jnp.int32)
    ang = ((k[:, None] * k[None, :]) % seq).astype(F32) * (2.0 * math.pi / seq)
    scale = 1.0 / math.sqrt(seq)
    return (jnp.cos(ang) * scale).astype(BF16), (jnp.sin(ang) * (-scale)).astype(BF16)


def _tdft_short(z16, a_s, tables, batch, seq, nb):
    cmat, smat_neg = tables
    n = batch * seq
    tn = COL_BLOCK
    out_shape = jax.ShapeDtypeStruct((n, FOURIER_WIDTH), BF16)
    return pl.pallas_call(
        functools.partial(_tdft_multi_kernel, nb=nb, seq=seq),
        grid=(batch // nb, FOURIER_WIDTH // tn),
        in_specs=[
            pl.BlockSpec((seq, seq), lambda b, c: (0, 0)),
            pl.BlockSpec((seq, seq), lambda b, c: (0, 0)),
            pl.BlockSpec((nb * seq, tn), lambda b, c: (b, SLOT_AC + c)),
            pl.BlockSpec((nb * seq, tn), lambda b, c: (b, c)),
        ],
        out_specs=pl.BlockSpec((nb * seq, tn), lambda b, c: (b, c)),
        out_shape=out_shape,
        compiler_params=_params("parallel", "parallel"),
        name="pos_dft_small",
    )(cmat, smat_neg, z16, a_s)


def _fmix_kernel(flo_ref, fhi_ref, sf_ref, wm_ref, o_ref, *, half_tiles):
    f = flo_ref[...]
    if half_tiles is not None:
        in_low_half = (pl.program_id(0) % (2 * half_tiles)) < half_tiles
        f = jnp.where(in_low_half, f, fhi_ref[...])
    mix = jnp.dot(f, wm_ref[...], preferred_element_type=F32)
    o_ref[...] = (mix * sf_ref[...].astype(F32)).astype(BF16)


def _fmix(f_lo, f_hi, z16, w_fmix, layer, n, tm, half_tiles):
    fw = FOURIER_WIDTH
    if half_tiles is None:
        lo_map = lambda i: (i, 0)
        hi_map = lambda i: (0, 0)
    else:
        def lo_map(i):
            b, r = i // (2 * half_tiles), i % (2 * half_tiles)
            return (b * half_tiles + jnp.minimum(r, half_tiles - 1), 0)

        def hi_map(i):
            b, r = i // (2 * half_tiles), i % (2 * half_tiles)
            return (b * half_tiles + jnp.maximum(r - half_tiles, 0), 0)
    return pl.pallas_call(
        functools.partial(_fmix_kernel, half_tiles=half_tiles),
        grid=(n // tm,),
        in_specs=[
            pl.BlockSpec((tm, fw), lo_map),
            pl.BlockSpec((tm, fw), hi_map),
            pl.BlockSpec((tm, fw), lambda i: (i, SLOT_SF * COL_BLOCK // fw)),
            pl.BlockSpec((None, fw, fw), lambda i: (layer, 0, 0)),
        ],
        out_specs=pl.BlockSpec((tm, fw), lambda i: (i, 0)),
        out_shape=jax.ShapeDtypeStruct((n, fw), BF16),
        compiler_params=_params("parallel"),
        name="fourier_mix",
    )(f_lo, f_hi, z16, w_fmix)


def _merge_kernel(yf_ref, yr_ref, wf_ref, wr_ref, gf_ref, gr_ref, o_ref):
    pf = jnp.dot(yf_ref[...], wf_ref[...], preferred_element_type=F32)
    pr = jnp.dot(yr_ref[...], wr_ref[...], preferred_element_type=F32)
    o_ref[...] = (gf_ref[...].astype(F32) * pf + gr_ref[...].astype(F32) * pr).astype(BF16)


def _merge(yf, yr, z16, w_bf, w_br, layer, tm):
    n = yr.shape[0]
    tn = COL_BLOCK
    return pl.pallas_call(
        _merge_kernel,
        grid=(n // tm, D_MODEL // tn),
        in_specs=[
            pl.BlockSpec((tm, FOURIER_WIDTH), lambda i, j: (i, 0)),
            pl.BlockSpec((tm, LRU_WIDTH), lambda i, j: (i, 0)),
            pl.BlockSpec((None, FOURIER_WIDTH, tn), lambda i, j: (layer, 0, j)),
            pl.BlockSpec((None, LRU_WIDTH, tn), lambda i, j: (layer, 0, j)),
            pl.BlockSpec((tm, tn), lambda i, j: (i, SLOT_SGF + j)),
            pl.BlockSpec((tm, tn), lambda i, j: (i, SLOT_SGR + j)),
        ],
        out_specs=pl.BlockSpec((tm, tn), lambda i, j: (i, j)),
        out_shape=jax.ShapeDtypeStruct((n, D_MODEL), BF16),
        compiler_params=_params("parallel", "parallel"),
        name="branch_merge",
    )(yf, yr, w_bf, w_br, z16, z16)


_OUT_COLS = D_MODEL // COL_BLOCK


def _unit_rms_rows(xrow):
    ssq = None
    for c in range(_OUT_COLS):
        v = xrow[c]
        part = jnp.sum(v * v, axis=-1, keepdims=True)
        ssq = part if ssq is None else ssq + part
    inv = lax.rsqrt(ssq * (1.0 / D_MODEL) + EPS)
    for c in range(_OUT_COLS):
        yield slice(c * COL_BLOCK, (c + 1) * COL_BLOCK), xrow[c] * inv


def _outproj_mid_kernel(m_ref, w_ref, x_ref, g_ref, ng_ref, sc_ref, sh_ref, xo_ref, h_ref, xrow):
    j = pl.program_id(1)
    xn = x_ref[...] + g_ref[...] * jnp.dot(m_ref[...], w_ref[...], preferred_element_type=F32)
    xo_ref[...] = xn
    xrow[j] = xn

    @pl.when(j == _OUT_COLS - 1)
    def _():
        gain = ng_ref[...] * (1.0 + sc_ref[...])
        for cols, y in _unit_rms_rows(xrow):
            h_ref[:, cols] = (y * gain[:, cols] + sh_ref[:, cols]).astype(BF16)


def _outproj_last_kernel(m_ref, w_ref, x_ref, g_ref, ng_ref, y_ref, xrow):
    j = pl.program_id(1)
    xrow[j] = x_ref[...] + g_ref[...] * jnp.dot(m_ref[...], w_ref[...], preferred_element_type=F32)

    @pl.when(j == _OUT_COLS - 1)
    def _():
        for cols, y in _unit_rms_rows(xrow):
            y_ref[:, cols] = y * ng_ref[:, cols]


def _outproj(merged, w_out, layer, x, gate, row_of, tm, norm_g, scale=None, shift=None):
    n = x.shape[0]
    tn = COL_BLOCK
    last = scale is None
    row_block = pl.BlockSpec((tm, D_MODEL), lambda i, j: (i, 0))
    col_block = pl.BlockSpec((tm, tn), lambda i, j: (i, j))
    mod_row = pl.BlockSpec((None, 1, D_MODEL), lambda i, j: (row_of(i), 0, 0))
    in_specs = [
        row_block,
        pl.BlockSpec((None, D_MODEL, tn), lambda i, j: (layer, 0, j)),
        col_block,
        pl.BlockSpec((None, 1, tn), lambda i, j: (row_of(i), 0, j)),
        pl.BlockSpec((1, D_MODEL), lambda i, j: (0, 0)),
    ]
    scratch = [pltpu.VMEM((_OUT_COLS, tm, tn), F32)]
    if last:
        return pl.pallas_call(
            _outproj_last_kernel,
            grid=(n // tm, _OUT_COLS),
            in_specs=in_specs,
            out_specs=row_block,
            out_shape=jax.ShapeDtypeStruct((n, D_MODEL), F32),
            scratch_shapes=scratch,
            compiler_params=_params("parallel", "arbitrary"),
            name="out_proj_final_norm",
        )(merged, w_out, x, gate, norm_g)
    return pl.pallas_call(
        _outproj_mid_kernel,
        grid=(n // tm, _OUT_COLS),
        in_specs=in_specs + [mod_row, mod_row],
        out_specs=[col_block, row_block],
        out_shape=[jax.ShapeDtypeStruct((n, D_MODEL), F32),
                   jax.ShapeDtypeStruct((n, D_MODEL), BF16)],
        scratch_shapes=scratch,
        compiler_params=_params("parallel", "arbitrary"),
        name="out_proj_norm",
    )(merged, w_out, x, gate, norm_g, scale, shift)


class _Tiles(NamedTuple):
    norm: int
    inproj: int
    merge: int
    outproj: int
    lru_channels: Optional[int]
    dft_rows: Optional[int]
    dft_batch: Optional[int]


_LONG_SEQ = 1024


def _tiles(seq):
    if seq >= _LONG_SEQ:
        return _Tiles(norm=512, inproj=1024, merge=1024, outproj=512, lru_channels=None,
                      dft_rows=512, dft_batch=None)
    return _Tiles(norm=512, inproj=1024, merge=1024, outproj=512, lru_channels=LRU_WIDTH,
                  dft_rows=None, dft_batch=8)


def _channel_dft_table():
    k = jnp.arange(FOURIER_GROUP_CH, dtype=jnp.int32)
    ang = ((k[:, None] * k[None, :]) % FOURIER_GROUP_CH).astype(F32) * (2.0 * math.pi / FOURIER_GROUP_CH)
    scale = 1.0 / math.sqrt(FOURIER_GROUP_CH)
    return jnp.concatenate([jnp.cos(ang) * scale, jnp.sin(ang) * scale], axis=1).astype(BF16)


def _per_head(v, parts):
    return (v.reshape(parts, LRU_HEADS, LRU_BLOCK).transpose(1, 0, 2)
            .reshape(LRU_HEADS, 1, parts * LRU_BLOCK))


def _stream(x, batch, seq, row_of_token_tile, h0_layers, mods, big, small, cs, final_g):
    n = batch * seq
    x = x.reshape(n, D_MODEL)
    tiles = _tiles(seq)
    long_seq = tiles.dft_rows is not None
    if long_seq:
        dft_tables = _tdft_sym_tables(seq, tiles.dft_rows)
        half_tiles = seq // 2 // tiles.merge
    else:
        dft_tables = _tdft_short_tables(seq)
        half_tiles = None
    states = []
    shift, scale, _ = mods[0]
    h = _norm_mod(x, small[0]["norm_g"], scale, shift, row_of_token_tile(tiles.norm), tiles.norm)
    for l in range(DEPTH):
        w = small[l]
        gate = mods[l][2]
        z16, a_s, rin = _inproj(h, big["w_in"], l, cs, tiles.inproj)
        if long_seq:
            f_lo, f_hi, yr = _tdft_lru(
                z16, a_s, rin, dft_tables, w["conv_w"], w["conv_b"], w["wg"], w["bg"], w["cq"],
                h0_layers[l], batch, seq, tiles.dft_rows)
        else:
            f_lo = f_hi = _tdft_short(z16, a_s, dft_tables, batch, seq, tiles.dft_batch)
            yr, st = _lru(rin, z16, w["conv_w"], w["conv_b"], w["wg"], w["bg"], w["cq"],
                          h0_layers[l], batch, seq, tiles.lru_channels)
            states.append(st)
        yf = _fmix(f_lo, f_hi, z16, big["w_fmix"], l, n, tiles.merge, half_tiles)
        merged = _merge(yf, yr, z16, big["w_bf"], big["w_br"], l, tiles.merge)
        row_of = row_of_token_tile(tiles.outproj)
        if l + 1 < DEPTH:
            next_shift, next_scale, _ = mods[l + 1]
            x, h = _outproj(merged, big["w_out"], l, x, gate, row_of, tiles.outproj,
                            small[l + 1]["norm_g"], next_scale, next_shift)
        else:
            y = _outproj(merged, big["w_out"], l, x, gate, row_of, tiles.outproj, final_g)
    return y.reshape(batch, seq, D_MODEL), states


def kernel(x_prompt, x_sample, state_lru, c, c_ctx, norm_g, w_ada, b_ada, w_in, w_fmix,
           conv_w, conv_b, w_gate_a, b_gate_a, w_gate_x, b_gate_x, lam,
           w_branch_f, w_branch_r, w_out, final_g):
    batch, seq, _ = x_prompt.shape
    dec_batch, dec_seq, _ = x_sample.shape
    assert 1 + dec_batch <= COND_ROWS

    cond = jnp.zeros((COND_ROWS, D_MODEL), F32)
    cond = cond.at[0].set(c_ctx).at[1:1 + dec_batch].set(c)
    mod = _ada(cond, w_ada, b_ada)
    mods = []
    for l in range(DEPTH):
        parts = jnp.split(mod[l], 3, axis=-1)
        mods.append(tuple(p.reshape(COND_ROWS, 1, D_MODEL) for p in parts))

    big = dict(w_in=w_in.astype(BF16), w_fmix=w_fmix.astype(BF16), w_bf=w_branch_f.astype(BF16),
               w_br=w_branch_r.astype(BF16), w_out=w_out.astype(BF16))
    small = []
    for l in range(DEPTH):
        wg = (0.5 * jnp.concatenate([w_gate_a[l, 0], w_gate_x[l, 0], w_gate_a[l, 1], w_gate_x[l, 1]],
                                    axis=-1)).astype(BF16)
        bg = _per_head(0.5 * jnp.stack([b_gate_a[l, 0], b_gate_x[l, 0], b_gate_a[l, 1], b_gate_x[l, 1]]), 4)
        cq = _per_head((0.25 * LRU_C) * jax.nn.softplus(-lam[l]), 2)
        small.append(dict(norm_g=norm_g[l].reshape(1, D_MODEL), conv_w=conv_w[l],
                          conv_b=conv_b[l].reshape(1, LRU_WIDTH), wg=wg, bg=bg, cq=cq))

    cs = _channel_dft_table()
    final_g = final_g.reshape(1, D_MODEL)

    zeros_h0 = jnp.zeros((batch, 2, LRU_WIDTH), F32)
    y_prompt, ctx_states = _stream(
        x_prompt, batch, seq, lambda tm: (lambda i: 0), [zeros_h0] * DEPTH,
        mods, big, small, cs, final_g)
    new_state = jnp.stack(ctx_states, axis=1)

    y_sample, _ = _stream(
        x_sample, dec_batch, dec_seq, lambda tm: (lambda i: 1 + (i * tm) // dec_seq),
        [state_lru[:, l] for l in range(DEPTH)], mods, big, small, cs, final_g)
    return (y_prompt, y_sample, new_state)
```

```python
import functools
import math
from typing import NamedTuple, Optional

import jax
import jax.numpy as jnp
from jax import lax
from jax.experimental import pallas as pl
from jax.experimental.pallas import tpu as pltpu

F32 = jnp.float32
BF16 = jnp.bfloat16

D_MODEL = 4096
DEPTH = 2
FOURIER_WIDTH = D_MODEL // 2
FOURIER_GROUP_CH = 256
LRU_WIDTH = D_MODEL // 2
LRU_BLOCK = 128
LRU_HEADS = LRU_WIDTH // LRU_BLOCK
CONV_WIDTH = 4
LRU_C = 8.0
IN_WIDTH = 2 * FOURIER_WIDTH + 2 * LRU_WIDTH + 2 * D_MODEL
EPS = 1e-6

SUBLANES = 8
LANES = 128
VMEM_LIMIT_BYTES = 58 * 1024 * 1024
F32_MIN_NORMAL = 2.0 ** -126

COND_ROWS = 16
COL_BLOCK = 1024
SLOT_AC, SLOT_SF, SLOT_SR, SLOT_SGF, SLOT_SGR = 0, 2, 4, 6, 10
PACKED_BLOCKS = 14


def _sigmoid(x):
    return 0.5 * jnp.tanh(0.5 * x) + 0.5


def _params(*sem):
    return pltpu.CompilerParams(dimension_semantics=sem, vmem_limit_bytes=VMEM_LIMIT_BYTES)


def _ada_kernel(c_ref, w_ref, b_ref, o_ref):
    c = c_ref[...]
    s = (c * _sigmoid(c)).astype(BF16)
    w = w_ref[...].astype(BF16)
    o_ref[...] = jnp.dot(s, w, preferred_element_type=F32) + b_ref[...]


def _ada(cond, w_ada, b_ada):
    tn = 512
    n_out = 3 * D_MODEL
    return pl.pallas_call(
        _ada_kernel,
        grid=(DEPTH, n_out // tn),
        in_specs=[
            pl.BlockSpec((COND_ROWS, D_MODEL), lambda l, j: (0, 0)),
            pl.BlockSpec((None, D_MODEL, tn), lambda l, j: (l, 0, j)),
            pl.BlockSpec((None, 1, tn), lambda l, j: (l, 0, j)),
        ],
        out_specs=pl.BlockSpec((None, COND_ROWS, tn), lambda l, j: (l, 0, j)),
        out_shape=jax.ShapeDtypeStruct((DEPTH, COND_ROWS, n_out), F32),
        compiler_params=_params("parallel", "parallel"),
        name="ada_mod",
    )(cond, w_ada, b_ada.reshape(DEPTH, 1, n_out))


def _norm_mod_kernel(x_ref, g_ref, sc_ref, sh_ref, o_ref):
    x = x_ref[...]
    ms = jnp.mean(x * x, axis=-1, keepdims=True)
    y = (x * lax.rsqrt(ms + EPS)) * g_ref[...]
    o_ref[...] = (y * (1.0 + sc_ref[...]) + sh_ref[...]).astype(o_ref.dtype)


def _norm_mod(x, g, scale, shift, row_of, tm):
    n = x.shape[0]
    return pl.pallas_call(
        _norm_mod_kernel,
        grid=(n // tm,),
        in_specs=[
            pl.BlockSpec((tm, D_MODEL), lambda i: (i, 0)),
            pl.BlockSpec((1, D_MODEL), lambda i: (0, 0)),
            pl.BlockSpec((None, 1, D_MODEL), lambda i: (row_of(i), 0, 0)),
            pl.BlockSpec((None, 1, D_MODEL), lambda i: (row_of(i), 0, 0)),
        ],
        out_specs=pl.BlockSpec((tm, D_MODEL), lambda i: (i, 0)),
        out_shape=jax.ShapeDtypeStruct((n, D_MODEL), BF16),
        compiler_params=_params("parallel"),
        name="norm_mod",
    )(x, g, scale, shift)


def _inproj_kernel(h_ref, w_ref, cs_ref, z16_ref, as_ref, rin_ref):
    j = pl.program_id(1)

    def z():
        return jnp.dot(h_ref[...], w_ref[...], preferred_element_type=F32)

    @pl.when(j < 2)
    def _():
        zb = z().astype(BF16)
        for g in range(COL_BLOCK // FOURIER_GROUP_CH):
            cols = slice(g * FOURIER_GROUP_CH, (g + 1) * FOURIER_GROUP_CH)
            t = jnp.dot(zb[:, cols], cs_ref[...], preferred_element_type=F32)
            z16_ref[:, cols] = t[:, :FOURIER_GROUP_CH].astype(BF16)
            as_ref[:, cols] = t[:, FOURIER_GROUP_CH:].astype(BF16)

    @pl.when(((j >= 2) & (j < 4)) | ((j >= 6) & (j < 8)))
    def _():
        v = z()
        z16_ref[...] = (v * _sigmoid(v)).astype(BF16)

    @pl.when((j >= 4) & (j < 6))
    def _():
        rin_ref[...] = z()

    @pl.when(j >= 8)
    def _():
        z16_ref[...] = _sigmoid(z()).astype(BF16)


def _packed_slot(j):
    return jnp.where(j < 4, j, jnp.where(j < 6, 3, j - 2))


def _inproj(h, w_in, layer, cs, tm):
    n = h.shape[0]
    tn = COL_BLOCK
    return pl.pallas_call(
        _inproj_kernel,
        grid=(n // tm, IN_WIDTH // tn),
        in_specs=[
            pl.BlockSpec((tm, D_MODEL), lambda i, j: (i, 0)),
            pl.BlockSpec((None, D_MODEL, tn), lambda i, j: (layer, 0, j)),
            pl.BlockSpec((FOURIER_GROUP_CH, 2 * FOURIER_GROUP_CH), lambda i, j: (0, 0)),
        ],
        out_specs=[
            pl.BlockSpec((tm, tn), lambda i, j: (i, _packed_slot(j))),
            pl.BlockSpec((tm, tn), lambda i, j: (i, jnp.minimum(j, 1))),
            pl.BlockSpec((tm, tn), lambda i, j: (i, jnp.clip(j - 4, 0, 1))),
        ],
        out_shape=[
            jax.ShapeDtypeStruct((n, PACKED_BLOCKS * COL_BLOCK), BF16),
            jax.ShapeDtypeStruct((n, FOURIER_WIDTH), BF16),
            jax.ShapeDtypeStruct((n, LRU_WIDTH), F32),
        ],
        compiler_params=_params("parallel", "arbitrary"),
        name="in_proj",
    )(h, w_in, cs)


_HALO = SUBLANES
_GATE_ROWS = 256
_SEGMENTS = SUBLANES
_SEG_PAD = SUBLANES


def _conv_taps(xe, cw_ref, cb_ref, rows):
    ext = rows + 2 * _HALO
    inner = slice(_HALO, _HALO + rows)
    xc = cb_ref[...] + pltpu.roll(xe, 1, 0)[inner] * cw_ref[0:1, :]
    xc = xc + xe[inner] * cw_ref[1:2, :]
    xc = xc + pltpu.roll(xe, ext - 1, 0)[inner] * cw_ref[2:3, :]
    return xc + pltpu.roll(xe, ext - 2, 0)[inner] * cw_ref[3:4, :]


def _gate_terms(gh, xh, cq):
    out = []
    for d in range(2):
        base = 2 * d * LRU_BLOCK
        t_a = jnp.tanh(gh[:, base:base + LRU_BLOCK])
        t_x = jnp.tanh(gh[:, base + LRU_BLOCK:base + 2 * LRU_BLOCK])
        c = cq[:, d * LRU_BLOCK:(d + 1) * LRU_BLOCK]
        s = jnp.tanh(c * t_a + c)
        inv = 1.0 / (1.0 + s)
        root = s * lax.rsqrt(jnp.maximum(s, F32_MIN_NORMAL))
        a = (1.0 - s) * inv
        u = (inv * root) * (xh * t_x + xh)
        out.append((a, u))
    return out


def _block_scan(a, b, row, reverse):
    for s in (1, 2, 4):
        if reverse:
            m, sh = row < SUBLANES - s, SUBLANES - s
        else:
            m, sh = row >= s, s
        a_sh = jnp.where(m, pltpu.roll(a, sh, 0), 1.0)
        b_sh = jnp.where(m, pltpu.roll(b, sh, 0), 0.0)
        b = b + a * b_sh
        a = a * a_sh
    return a, b


_SCAN_ROWS = 2 * SUBLANES


def _scan_both(a_f, u_f, a_b, u_b, h_f, h_b, h0_ref, st_ref, *, seq, tc, cs):
    row = lax.broadcasted_iota(jnp.int32, (SUBLANES, cs), 0)
    for c0 in range(0, tc, cs):
        cols = slice(c0, c0 + cs)

        def body(k, carry, cols=cols):
            carry_f, carry_b = carry
            rf = pl.multiple_of(k * _SCAN_ROWS, _SCAN_ROWS)
            rb = pl.multiple_of(seq - _SCAN_ROWS - k * _SCAN_ROWS, _SCAN_ROWS)
            a2 = a_f[pl.ds(rf, _SCAN_ROWS), cols]
            b2 = u_f[pl.ds(rf, _SCAN_ROWS), cols].astype(F32)
            hs = []
            for p in (0, 1):
                blk = slice(p * SUBLANES, (p + 1) * SUBLANES)
                a, b = _block_scan(a2[blk], b2[blk], row, False)
                h = a * carry_f + b
                hs.append(h)
                carry_f = jnp.broadcast_to(h[SUBLANES - 1:SUBLANES, :], (SUBLANES, cs))
            h_f[pl.ds(rf, _SCAN_ROWS), cols] = jnp.concatenate(hs, axis=0)

            a2 = a_b[pl.ds(rb, _SCAN_ROWS), cols]
            b2 = u_b[pl.ds(rb, _SCAN_ROWS), cols].astype(F32)
            hs = [None, None]
            for p in (1, 0):
                blk = slice(p * SUBLANES, (p + 1) * SUBLANES)
                a, b = _block_scan(a2[blk], b2[blk], row, True)
                h = a * carry_b + b
                hs[p] = h
                carry_b = jnp.broadcast_to(h[0:1, :], (SUBLANES, cs))
            h_b[pl.ds(rb, _SCAN_ROWS), cols] = jnp.concatenate(hs, axis=0)
            return carry_f, carry_b

        init = (jnp.broadcast_to(h0_ref[0:1, cols], (SUBLANES, cs)),
                jnp.broadcast_to(h0_ref[1:2, cols], (SUBLANES, cs)))
        fin_f, fin_b = lax.fori_loop(0, seq // _SCAN_ROWS, body, init)
        st_ref[0:1, cols] = fin_f[0:1, :]
        st_ref[1:2, cols] = fin_b[0:1, :]


def _write_gated_sum(h_f, h_b, sr_ref, y_ref, seq):
    def body(c, carry):
        r = pl.ds(pl.multiple_of(c * _GATE_ROWS, _GATE_ROWS), _GATE_ROWS)
        y_ref[r, :] = ((h_f[r, :] + h_b[r, :]) * sr_ref[r, :].astype(F32)).astype(BF16)
        return carry
    lax.fori_loop(0, seq // _GATE_ROWS, body, 0)


def _lru_kernel(rin_ref, sr_ref, cw_ref, cb_ref, wg_ref, bg_ref, cq_ref, h0_ref,
                y_ref, st_ref, xpad, a_f, u_f, a_b, u_b, *, seq, tc, cs):
    heads = tc // LRU_BLOCK
    n_chunks = seq // _GATE_ROWS

    def rows(c):
        return pl.multiple_of(c * _GATE_ROWS, _GATE_ROWS)

    zero_halo = jnp.zeros((_HALO, tc), F32)
    xpad[0:_HALO, :] = zero_halo
    xpad[seq + _HALO:seq + 2 * _HALO, :] = zero_halo

    def copy_body(c, carry):
        r0 = rows(c)
        xpad[pl.ds(r0 + _HALO, _GATE_ROWS), :] = rin_ref[pl.ds(r0, _GATE_ROWS), :]
        return carry
    lax.fori_loop(0, n_chunks, copy_body, 0)

    def gate_body(c, carry):
        r0 = rows(c)
        xc = _conv_taps(xpad[pl.ds(r0, _GATE_ROWS + 2 * _HALO), :], cw_ref, cb_ref, _GATE_ROWS)
        for hh in range(heads):
            cols = slice(hh * LRU_BLOCK, (hh + 1) * LRU_BLOCK)
            xh = xc[:, cols]
            gh = jnp.dot(xh.astype(BF16), wg_ref[hh], preferred_element_type=F32) + bg_ref[hh]
            (af, uf), (ab, ub) = _gate_terms(gh, xh, cq_ref[hh])
            a_f[pl.ds(r0, _GATE_ROWS), cols] = af
            u_f[pl.ds(r0, _GATE_ROWS), cols] = uf
            a_b[pl.ds(r0, _GATE_ROWS), cols] = ab
            u_b[pl.ds(r0, _GATE_ROWS), cols] = ub
        return carry
    lax.fori_loop(0, n_chunks, gate_body, 0)

    _scan_both(a_f, u_f, a_b, u_b, a_f, a_b, h0_ref, st_ref, seq=seq, tc=tc, cs=cs)
    _write_gated_sum(a_f, a_b, sr_ref, y_ref, seq)


def _lru(rin, z16, conv_w, conv_b, wg, bg, cq, h0, batch, seq, tc):
    n = batch * seq
    heads = tc // LRU_BLOCK
    cs = min(tc, 512)
    sr_block0 = SLOT_SR * COL_BLOCK // tc
    kern = functools.partial(_lru_kernel, seq=seq, tc=tc, cs=cs)
    return pl.pallas_call(
        kern,
        grid=(batch, LRU_WIDTH // tc),
        in_specs=[
            pl.BlockSpec((seq, tc), lambda b, c: (b, c)),
            pl.BlockSpec((seq, tc), lambda b, c: (b, sr_block0 + c)),
            pl.BlockSpec((CONV_WIDTH, tc), lambda b, c: (0, c)),
            pl.BlockSpec((1, tc), lambda b, c: (0, c)),
            pl.BlockSpec((heads, LRU_BLOCK, 4 * LRU_BLOCK), lambda b, c: (c, 0, 0)),
            pl.BlockSpec((heads, 1, 4 * LRU_BLOCK), lambda b, c: (c, 0, 0)),
            pl.BlockSpec((heads, 1, 2 * LRU_BLOCK), lambda b, c: (c, 0, 0)),
            pl.BlockSpec((None, 2, tc), lambda b, c: (b, 0, c)),
        ],
        out_specs=[
            pl.BlockSpec((seq, tc), lambda b, c: (b, c)),
            pl.BlockSpec((None, 2, tc), lambda b, c: (b, 0, c)),
        ],
        out_shape=[
            jax.ShapeDtypeStruct((n, LRU_WIDTH), BF16),
            jax.ShapeDtypeStruct((batch, 2, LRU_WIDTH), F32),
        ],
        scratch_shapes=[
            pltpu.VMEM((seq + 2 * _HALO, tc), F32),
            pltpu.VMEM((seq, tc), F32),
            pltpu.VMEM((seq, tc), F32),
            pltpu.VMEM((seq, tc), F32),
            pltpu.VMEM((seq, tc), F32),
        ],
        compiler_params=_params("parallel", "parallel"),
        name="conv_rglru",
    )(rin, z16, conv_w, conv_b, wg, bg, cq, h0)


_MIRROR_PAD = 16


def _tdft_lru_kernel(c_ref, s_ref, ac_ref, as_ref, sel_ref, rin_ref, cw_ref, cb_ref, wg_ref,
                     bg_ref, cq_ref, sr_ref, h0_ref, lo_ref, hi_ref, y_ref,
                     xs, a_f, u_f, a_b, u_b, *, tm, seq):
    seg = seq // _SEGMENTS
    pitch = seg + _SEG_PAD
    for s in range(_SEGMENTS):
        xs[s * pitch:s * pitch + seg, :] = rin_ref[s * seg:(s + 1) * seg, :]
    row = lax.broadcasted_iota(jnp.int32, (_SEGMENTS, LRU_BLOCK), 0)

    def positions(j):
        if 0 <= j < seg:
            return xs[pl.ds(j, _SEGMENTS, stride=pitch), :]
        if j < 0:
            v = xs[pl.ds(seg + j, _SEGMENTS, stride=pitch), :]
            return jnp.where(row >= 1, pltpu.roll(v, 1, 0), 0.0)
        v = xs[pl.ds(j - seg, _SEGMENTS, stride=pitch), :]
        return jnp.where(row < _SEGMENTS - 1, pltpu.roll(v, _SEGMENTS - 1, 0), 0.0)

    taps = [jnp.broadcast_to(cw_ref[i:i + 1, :], (_SEGMENTS, LRU_BLOCK)) for i in range(CONV_WIDTH)]
    bias = jnp.broadcast_to(cb_ref[...], (_SEGMENTS, LRU_BLOCK))
    per_chunk = _GATE_ROWS // _SEGMENTS

    n_chunks = seq // _GATE_ROWS
    blocks = [slice(jj * _SEGMENTS, (jj + 1) * _SEGMENTS) for jj in range(per_chunk)]

    def chunk_rows(k):
        return slice(k * _GATE_ROWS, (k + 1) * _GATE_ROWS)

    def chunk_slabs(k):
        return [slice(s * seg + k * per_chunk, s * seg + (k + 1) * per_chunk)
                for s in range(_SEGMENTS)]

    def dft_slice(k, acc):
        r = chunk_rows(k)
        dp = jnp.dot(c_ref[:, r], ac_ref[r, :], preferred_element_type=F32)
        dq = jnp.dot(s_ref[:, r], as_ref[r, :], preferred_element_type=F32)
        return (dp, dq) if acc is None else (acc[0] + dp, acc[1] + dq)

    zeros = jnp.zeros((_SEGMENTS, LRU_BLOCK), F32)
    ones = jnp.ones((_SEGMENTS, LRU_BLOCK), F32)
    end_f, prod_f, end_b, prod_b = zeros, ones, zeros, ones
    acc = None
    for k in range(n_chunks):
        r = chunk_rows(k)
        j0 = k * per_chunk
        vs = [positions(j) for j in range(j0 - 1, j0 + per_chunk + CONV_WIDTH - 2)]
        xc = jnp.concatenate(
            [((bias + vs[jj] * taps[0]) + vs[jj + 1] * taps[1] + vs[jj + 2] * taps[2])
             + vs[jj + 3] * taps[3] for jj in range(per_chunk)], axis=0)
        gh = jnp.dot(xc.astype(BF16), wg_ref[...], preferred_element_type=F32) + bg_ref[...]
        acc = dft_slice(k, acc)

        (af, uf), (ab, ub) = _gate_terms(gh, xc, cq_ref[...])
        a_f[r, :] = af
        u_f[r, :] = uf
        a_b[r, :] = ab
        u_b[r, :] = ub
        for blk in blocks:
            end_f = af[blk] * end_f + uf[blk]
            prod_f = prod_f * af[blk]
            end_b = end_b + prod_b * ub[blk]
            prod_b = prod_b * ab[blk]

    p, q = acc
    lo_ref[...] = (p[:tm] + q[:tm]).astype(BF16)
    mirrored = (p - q).astype(BF16)
    hi_ref[...] = jnp.dot(sel_ref[...], mirrored, preferred_element_type=F32).astype(BF16)

    h0_f = jnp.broadcast_to(h0_ref[0:1, :], (_SEGMENTS, LRU_BLOCK))
    h0_b = jnp.broadcast_to(h0_ref[1:2, :], (_SEGMENTS, LRU_BLOCK))
    a_cum, b_cum = _block_scan(prod_f, end_f, row, False)
    true_end_f = a_cum * h0_f + b_cum
    h_fwd = jnp.where(row >= 1, pltpu.roll(true_end_f, 1, 0), h0_f)
    a_cum, b_cum = _block_scan(prod_b, end_b, row, True)
    true_end_b = a_cum * h0_b + b_cum
    h_bwd = jnp.where(row < _SEGMENTS - 1, pltpu.roll(true_end_b, _SEGMENTS - 1, 0), h0_b)

    for s in range(_SEGMENTS):
        xs[s * pitch:s * pitch + seg, :] = sr_ref[s * seg:(s + 1) * seg, :].astype(F32)

    def emit_y(k):
        r = chunk_rows(k)
        j0 = k * per_chunk
        sr_seg = jnp.concatenate(
            [xs[pl.ds(j, _SEGMENTS, stride=pitch), :] for j in range(j0, j0 + per_chunk)], axis=0)
        u_f[r, :] = (u_f[r, :] + u_b[r, :]) * sr_seg
        for s, sl in enumerate(chunk_slabs(k)):
            groups = [u_f[pl.ds(k * _GATE_ROWS + g * _SEGMENTS * _SEGMENTS + s, _SEGMENTS,
                                stride=_SEGMENTS), :] for g in range(per_chunk // _SEGMENTS)]
            y_ref[sl, :] = jnp.concatenate(groups, axis=0).astype(BF16)

    half_chunks = n_chunks // 2
    for k in range(n_chunks):
        rf, rb = chunk_rows(k), chunk_rows(n_chunks - 1 - k)
        af_c, uf_c, ab_c, ub_c = a_f[rf, :], u_f[rf, :], a_b[rb, :], u_b[rb, :]
        hfs, hbs = [], [None] * per_chunk
        for jj in range(per_chunk):
            h_fwd = af_c[blocks[jj]] * h_fwd + uf_c[blocks[jj]]
            hfs.append(h_fwd)
            jb = per_chunk - 1 - jj
            h_bwd = ab_c[blocks[jb]] * h_bwd + ub_c[blocks[jb]]
            hbs[jb] = h_bwd
        u_f[rf, :] = jnp.concatenate(hfs, axis=0)
        u_b[rb, :] = jnp.concatenate(hbs, axis=0)
        if k >= half_chunks:
            emit_y(k)
            emit_y(n_chunks - 1 - k)


def _tdft_sym_tables(seq, tm):
    half_tiles = seq // 2 // tm
    ext = tm + _MIRROR_PAD
    k = (jnp.arange(half_tiles, dtype=jnp.int32)[:, None] * tm
         + jnp.arange(ext, dtype=jnp.int32)[None, :]).reshape(-1)
    side = math.isqrt(seq)
    assert side * side == seq
    tt = jnp.arange(side, dtype=jnp.int32)
    coarse = ((k[:, None] * tt[None, :]) % side).astype(F32) * (2.0 * math.pi / side)
    fine = ((k[:, None] * tt[None, :]) % seq).astype(F32) * (2.0 * math.pi / seq)
    t = jnp.arange(seq, dtype=jnp.int32)
    pick_coarse = (tt[:, None] == (t // side)[None, :]).astype(F32)
    pick_fine = (tt[:, None] == (t % side)[None, :]).astype(F32)
    expand = functools.partial(jnp.dot, precision=lax.Precision.HIGHEST)
    ca, sa = expand(jnp.cos(coarse), pick_coarse), expand(jnp.sin(coarse), pick_coarse)
    cb, sb = expand(jnp.cos(fine), pick_fine), expand(jnp.sin(fine), pick_fine)
    scale = 1.0 / math.sqrt(seq)
    cmat = ((ca * cb - sa * sb) * scale).astype(BF16).reshape(half_tiles, ext, seq)
    smat_neg = ((sa * cb + ca * sb) * (-scale)).astype(BF16).reshape(half_tiles, ext, seq)
    r = jnp.arange(tm, dtype=jnp.int32)
    sel = (jnp.arange(ext, dtype=jnp.int32)[None, :] == (tm - r)[:, None]).astype(BF16)
    return cmat, smat_neg, sel


def _tdft_lru(z16, a_s, rin, tables, conv_w, conv_b, wg, bg, cq, h0, batch, seq, tm):
    cmat, smat_neg, sel = tables
    tn = COL_BLOCK // 2
    ac_block0 = SLOT_AC * COL_BLOCK // tn
    sr_block0 = SLOT_SR * COL_BLOCK // LRU_BLOCK
    half_tiles = seq // 2 // tm
    col_tiles = FOURIER_WIDTH // tn
    assert col_tiles * half_tiles == LRU_HEADS
    ext = tm + _MIRROR_PAD
    n = batch * seq
    half = jax.ShapeDtypeStruct((n // 2, FOURIER_WIDTH), BF16)

    def head(c, i):
        return c * half_tiles + i

    head_cols = pl.BlockSpec((seq, LRU_BLOCK), lambda i, b, c: (b, head(c, i)))
    head_f32 = pltpu.VMEM((seq, LRU_BLOCK), F32)
    return pl.pallas_call(
        functools.partial(_tdft_lru_kernel, tm=tm, seq=seq),
        grid=(half_tiles, batch, col_tiles),
        in_specs=[
            pl.BlockSpec((None, ext, seq), lambda i, b, c: (i, 0, 0)),
            pl.BlockSpec((None, ext, seq), lambda i, b, c: (i, 0, 0)),
            pl.BlockSpec((seq, tn), lambda i, b, c: (b, ac_block0 + c)),
            pl.BlockSpec((seq, tn), lambda i, b, c: (b, c)),
            pl.BlockSpec((tm, ext), lambda i, b, c: (0, 0)),
            head_cols,
            pl.BlockSpec((CONV_WIDTH, LRU_BLOCK), lambda i, b, c: (0, head(c, i))),
            pl.BlockSpec((1, LRU_BLOCK), lambda i, b, c: (0, head(c, i))),
            pl.BlockSpec((None, LRU_BLOCK, 4 * LRU_BLOCK), lambda i, b, c: (head(c, i), 0, 0)),
            pl.BlockSpec((None, 1, 4 * LRU_BLOCK), lambda i, b, c: (head(c, i), 0, 0)),
            pl.BlockSpec((None, 1, 2 * LRU_BLOCK), lambda i, b, c: (head(c, i), 0, 0)),
            pl.BlockSpec((seq, LRU_BLOCK), lambda i, b, c: (b, sr_block0 + head(c, i))),
            pl.BlockSpec((None, 2, LRU_BLOCK), lambda i, b, c: (b, 0, head(c, i))),
        ],
        out_specs=[
            pl.BlockSpec((tm, tn), lambda i, b, c: (b * half_tiles + i, c)),
            pl.BlockSpec((tm, tn), lambda i, b, c: (b * half_tiles + half_tiles - 1 - i, c)),
            head_cols,
        ],
        out_shape=[half, half, jax.ShapeDtypeStruct((n, LRU_WIDTH), BF16)],
        scratch_shapes=[
            pltpu.VMEM((_SEGMENTS * (seq // _SEGMENTS + _SEG_PAD), LRU_BLOCK), F32),
            head_f32, head_f32, head_f32, head_f32,
        ],
        compiler_params=_params("parallel", "parallel", "parallel"),
        name="pos_dft_rglru",
    )(cmat, smat_neg, z16, a_s, sel, rin, conv_w, conv_b, wg, bg, cq, z16, h0)


def _tdft_multi_kernel(c_ref, s_ref, ac_ref, as_ref, o_ref, *, nb, seq):
    for b in range(nb):
        r = slice(b * seq, (b + 1) * seq)
        o_ref[r, :] = (jnp.dot(c_ref[...], ac_ref[r, :], preferred_element_type=F32)
                       + jnp.dot(s_ref[...], as_ref[r, :], preferred_element_type=F32)).astype(BF16)


def _tdft_short_tables(seq):
    k = jnp.arange(seq, dtype=jnp.int32)
    ang = ((k[:, None] * k[None, :]) % seq).astype(F32) * (2.0 * math.pi / seq)
    scale = 1.0 / math.sqrt(seq)
    return (jnp.cos(ang) * scale).astype(BF16), (jnp.sin(ang) * (-scale)).astype(BF16)


def _tdft_short(z16, a_s, tables, batch, seq, nb):
    cmat, smat_neg = tables
    n = batch * seq
    tn = COL_BLOCK
    out_shape = jax.ShapeDtypeStruct((n, FOURIER_WIDTH), BF16)
    return pl.pallas_call(
        functools.partial(_tdft_multi_kernel, nb=nb, seq=seq),
        grid=(batch // nb, FOURIER_WIDTH // tn),
        in_specs=[
            pl.BlockSpec((seq, seq), lambda b, c: (0, 0)),
            pl.BlockSpec((seq, seq), lambda b, c: (0, 0)),
            pl.BlockSpec((nb * seq, tn), lambda b, c: (b, SLOT_AC + c)),
            pl.BlockSpec((nb * seq, tn), lambda b, c: (b, c)),
        ],
        out_specs=pl.BlockSpec((nb * seq, tn), lambda b, c: (b, c)),
        out_shape=out_shape,
        compiler_params=_params("parallel", "parallel"),
        name="pos_dft_small",
    )(cmat, smat_neg, z16, a_s)


def _fmix_kernel(flo_ref, fhi_ref, sf_ref, wm_ref, o_ref, *, half_tiles):
    f = flo_ref[...]
    if half_tiles is not None:
        in_low_half = (pl.program_id(0) % (2 * half_tiles)) < half_tiles
        f = jnp.where(in_low_half, f, fhi_ref[...])
    mix = jnp.dot(f, wm_ref[...], preferred_element_type=F32)
    o_ref[...] = (mix * sf_ref[...].astype(F32)).astype(BF16)


def _fmix(f_lo, f_hi, z16, w_fmix, layer, n, tm, half_tiles):
    fw = FOURIER_WIDTH
    if half_tiles is None:
        lo_map = lambda i: (i, 0)
        hi_map = lambda i: (0, 0)
    else:
        def lo_map(i):
            b, r = i // (2 * half_tiles), i % (2 * half_tiles)
            return (b * half_tiles + jnp.minimum(r, half_tiles - 1), 0)

        def hi_map(i):
            b, r = i // (2 * half_tiles), i % (2 * half_tiles)
            return (b * half_tiles + jnp.maximum(r - half_tiles, 0), 0)
    return pl.pallas_call(
        functools.partial(_fmix_kernel, half_tiles=half_tiles),
        grid=(n // tm,),
        in_specs=[
            pl.BlockSpec((tm, fw), lo_map),
            pl.BlockSpec((tm, fw), hi_map),
            pl.BlockSpec((tm, fw), lambda i: (i, SLOT_SF * COL_BLOCK // fw)),
            pl.BlockSpec((None, fw, fw), lambda i: (layer, 0, 0)),
        ],
        out_specs=pl.BlockSpec((tm, fw), lambda i: (i, 0)),
        out_shape=jax.ShapeDtypeStruct((n, fw), BF16),
        compiler_params=_params("parallel"),
        name="fourier_mix",
    )(f_lo, f_hi, z16, w_fmix)


def _merge_kernel(yf_ref, yr_ref, wf_ref, wr_ref, gf_ref, gr_ref, o_ref):
    pf = jnp.dot(yf_ref[...], wf_ref[...], preferred_element_type=F32)
    pr = jnp.dot(yr_ref[...], wr_ref[...], preferred_element_type=F32)
    o_ref[...] = (gf_ref[...].astype(F32) * pf + gr_ref[...].astype(F32) * pr).astype(BF16)


def _merge(yf, yr, z16, w_bf, w_br, layer, tm):
    n = yr.shape[0]
    tn = COL_BLOCK
    return pl.pallas_call(
        _merge_kernel,
        grid=(n // tm, D_MODEL // tn),
        in_specs=[
            pl.BlockSpec((tm, FOURIER_WIDTH), lambda i, j: (i, 0)),
            pl.BlockSpec((tm, LRU_WIDTH), lambda i, j: (i, 0)),
            pl.BlockSpec((None, FOURIER_WIDTH, tn), lambda i, j: (layer, 0, j)),
            pl.BlockSpec((None, LRU_WIDTH, tn), lambda i, j: (layer, 0, j)),
            pl.BlockSpec((tm, tn), lambda i, j: (i, SLOT_SGF + j)),
            pl.BlockSpec((tm, tn), lambda i, j: (i, SLOT_SGR + j)),
        ],
        out_specs=pl.BlockSpec((tm, tn), lambda i, j: (i, j)),
        out_shape=jax.ShapeDtypeStruct((n, D_MODEL), BF16),
        compiler_params=_params("parallel", "parallel"),
        name="branch_merge",
    )(yf, yr, w_bf, w_br, z16, z16)


_OUT_COLS = D_MODEL // COL_BLOCK


def _unit_rms_rows(xrow):
    ssq = None
    for c in range(_OUT_COLS):
        v = xrow[c]
        part = jnp.sum(v * v, axis=-1, keepdims=True)
        ssq = part if ssq is None else ssq + part
    inv = lax.rsqrt(ssq * (1.0 / D_MODEL) + EPS)
    for c in range(_OUT_COLS):
        yield slice(c * COL_BLOCK, (c + 1) * COL_BLOCK), xrow[c] * inv


def _outproj_mid_kernel(m_ref, w_ref, x_ref, g_ref, ng_ref, sc_ref, sh_ref, xo_ref, h_ref, xrow):
    j = pl.program_id(1)
    xn = x_ref[...] + g_ref[...] * jnp.dot(m_ref[...], w_ref[...], preferred_element_type=F32)
    xo_ref[...] = xn
    xrow[j] = xn

    @pl.when(j == _OUT_COLS - 1)
    def _():
        gain = ng_ref[...] * (1.0 + sc_ref[...])
        for cols, y in _unit_rms_rows(xrow):
            h_ref[:, cols] = (y * gain[:, cols] + sh_ref[:, cols]).astype(BF16)


def _outproj_last_kernel(m_ref, w_ref, x_ref, g_ref, ng_ref, y_ref, xrow):
    j = pl.program_id(1)
    xrow[j] = x_ref[...] + g_ref[...] * jnp.dot(m_ref[...], w_ref[...], preferred_element_type=F32)

    @pl.when(j == _OUT_COLS - 1)
    def _():
        for cols, y in _unit_rms_rows(xrow):
            y_ref[:, cols] = y * ng_ref[:, cols]


def _outproj(merged, w_out, layer, x, gate, row_of, tm, norm_g, scale=None, shift=None):
    n = x.shape[0]
    tn = COL_BLOCK
    last = scale is None
    row_block = pl.BlockSpec((tm, D_MODEL), lambda i, j: (i, 0))
    col_block = pl.BlockSpec((tm, tn), lambda i, j: (i, j))
    mod_row = pl.BlockSpec((None, 1, D_MODEL), lambda i, j: (row_of(i), 0, 0))
    in_specs = [
        row_block,
        pl.BlockSpec((None, D_MODEL, tn), lambda i, j: (layer, 0, j)),
        col_block,
        pl.BlockSpec((None, 1, tn), lambda i, j: (row_of(i), 0, j)),
        pl.BlockSpec((1, D_MODEL), lambda i, j: (0, 0)),
    ]
    scratch = [pltpu.VMEM((_OUT_COLS, tm, tn), F32)]
    if last:
        return pl.pallas_call(
            _outproj_last_kernel,
            grid=(n // tm, _OUT_COLS),
            in_specs=in_specs,
            out_specs=row_block,
            out_shape=jax.ShapeDtypeStruct((n, D_MODEL), F32),
            scratch_shapes=scratch,
            compiler_params=_params("parallel", "arbitrary"),
            name="out_proj_final_norm",
        )(merged, w_out, x, gate, norm_g)
    return pl.pallas_call(
        _outproj_mid_kernel,
        grid=(n // tm, _OUT_COLS),
        in_specs=in_specs + [mod_row, mod_row],
        out_specs=[col_block, row_block],
        out_shape=[jax.ShapeDtypeStruct((n, D_MODEL), F32),
                   jax.ShapeDtypeStruct((n, D_MODEL), BF16)],
        scratch_shapes=scratch,
        compiler_params=_params("parallel", "arbitrary"),
        name="out_proj_norm",
    )(merged, w_out, x, gate, norm_g, scale, shift)


class _Tiles(NamedTuple):
    norm: int
    inproj: int
    merge: int
    outproj: int
    lru_channels: Optional[int]
    dft_rows: Optional[int]
    dft_batch: Optional[int]


_LONG_SEQ = 1024


def _tiles(seq):
    if seq >= _LONG_SEQ:
        return _Tiles(norm=512, inproj=1024, merge=1024, outproj=512, lru_channels=None,
                      dft_rows=512, dft_batch=None)
    return _Tiles(norm=512, inproj=1024, merge=1024, outproj=512, lru_channels=LRU_WIDTH,
                  dft_rows=None, dft_batch=8)


def _channel_dft_table():
    k = jnp.arange(FOURIER_GROUP_CH, dtype=jnp.int32)
    ang = ((k[:, None] * k[None, :]) % FOURIER_GROUP_CH).astype(F32) * (2.0 * math.pi / FOURIER_GROUP_CH)
    scale = 1.0 / math.sqrt(FOURIER_GROUP_CH)
    return jnp.concatenate([jnp.cos(ang) * scale, jnp.sin(ang) * scale], axis=1).astype(BF16)


def _per_head(v, parts):
    return (v.reshape(parts, LRU_HEADS, LRU_BLOCK).transpose(1, 0, 2)
            .reshape(LRU_HEADS, 1, parts * LRU_BLOCK))


def _stream(x, batch, seq, row_of_token_tile, h0_layers, mods, big, small, cs, final_g):
    n = batch * seq
    x = x.reshape(n, D_MODEL)
    tiles = _tiles(seq)
    long_seq = tiles.dft_rows is not None
    if long_seq:
        dft_tables = _tdft_sym_tables(seq, tiles.dft_rows)
        half_tiles = seq // 2 // tiles.merge
    else:
        dft_tables = _tdft_short_tables(seq)
        half_tiles = None
    states = []
    shift, scale, _ = mods[0]
    h = _norm_mod(x, small[0]["norm_g"], scale, shift, row_of_token_tile(tiles.norm), tiles.norm)
    for l in range(DEPTH):
        w = small[l]
        gate = mods[l][2]
        z16, a_s, rin = _inproj(h, big["w_in"], l, cs, tiles.inproj)
        if long_seq:
            f_lo, f_hi, yr = _tdft_lru(
                z16, a_s, rin, dft_tables, w["conv_w"], w["conv_b"], w["wg"], w["bg"], w["cq"],
                h0_layers[l], batch, seq, tiles.dft_rows)
        else:
            f_lo = f_hi = _tdft_short(z16, a_s, dft_tables, batch, seq, tiles.dft_batch)
            yr, st = _lru(rin, z16, w["conv_w"], w["conv_b"], w["wg"], w["bg"], w["cq"],
                          h0_layers[l], batch, seq, tiles.lru_channels)
            states.append(st)
        yf = _fmix(f_lo, f_hi, z16, big["w_fmix"], l, n, tiles.merge, half_tiles)
        merged = _merge(yf, yr, z16, big["w_bf"], big["w_br"], l, tiles.merge)
        row_of = row_of_token_tile(tiles.outproj)
        if l + 1 < DEPTH:
            next_shift, next_scale, _ = mods[l + 1]
            x, h = _outproj(merged, big["w_out"], l, x, gate, row_of, tiles.outproj,
                            small[l + 1]["norm_g"], next_scale, next_shift)
        else:
            y = _outproj(merged, big["w_out"], l, x, gate, row_of, tiles.outproj, final_g)
    return y.reshape(batch, seq, D_MODEL), states


def kernel(x_prompt, x_sample, state_lru, c, c_ctx, norm_g, w_ada, b_ada, w_in, w_fmix,
           conv_w, conv_b, w_gate_a, b_gate_a, w_gate_x, b_gate_x, lam,
           w_branch_f, w_branch_r, w_out, final_g):
    batch, seq, _ = x_prompt.shape
    dec_batch, dec_seq, _ = x_sample.shape
    assert 1 + dec_batch <= COND_ROWS

    cond = jnp.zeros((COND_ROWS, D_MODEL), F32)
    cond = cond.at[0].set(c_ctx).at[1:1 + dec_batch].set(c)
    mod = _ada(cond, w_ada, b_ada)
    mods = []
    for l in range(DEPTH):
        parts = jnp.split(mod[l], 3, axis=-1)
        mods.append(tuple(p.reshape(COND_ROWS, 1, D_MODEL) for p in parts))

    big = dict(w_in=w_in.astype(BF16), w_fmix=w_fmix.astype(BF16), w_bf=w_branch_f.astype(BF16),
               w_br=w_branch_r.astype(BF16), w_out=w_out.astype(BF16))
    small = []
    for l in range(DEPTH):
        wg = (0.5 * jnp.concatenate([w_gate_a[l, 0], w_gate_x[l, 0], w_gate_a[l, 1], w_gate_x[l, 1]],
                                    axis=-1)).astype(BF16)
        bg = _per_head(0.5 * jnp.stack([b_gate_a[l, 0], b_gate_x[l, 0], b_gate_a[l, 1], b_gate_x[l, 1]]), 4)
        cq = _per_head((0.25 * LRU_C) * jax.nn.softplus(-lam[l]), 2)
        small.append(dict(norm_g=norm_g[l].reshape(1, D_MODEL), conv_w=conv_w[l],
                          conv_b=conv_b[l].reshape(1, LRU_WIDTH), wg=wg, bg=bg, cq=cq))

    cs = _channel_dft_table()
    final_g = final_g.reshape(1, D_MODEL)

    zeros_h0 = jnp.zeros((batch, 2, LRU_WIDTH), F32)
    y_prompt, ctx_states = _stream(
        x_prompt, batch, seq, lambda tm: (lambda i: 0), [zeros_h0] * DEPTH,
        mods, big, small, cs, final_g)
    new_state = jnp.stack(ctx_states, axis=1)

    y_sample, _ = _stream(
        x_sample, dec_batch, dec_seq, lambda tm: (lambda i: 1 + (i * tm) // dec_seq),
        [state_lru[:, l] for l in range(DEPTH)], mods, big, small, cs, final_g)
    return (y_prompt, y_sample, new_state)
```
